```python
import jax, jax.numpy as jnp
from jax import lax
import numpy as np

D_MODEL = 2048
BATCH = 2
SEQ = 4096
DEPTH = 1
DEC_BATCH = 32
DEC_SEQ = 1
PAST_LEN = 8192
PAGE_SIZE = 128

RWKV_HEADS = 16
RWKV_HEAD_DIM = 64
RWKV_WIDTH = RWKV_HEADS * RWKV_HEAD_DIM
DECAY_LORA = 96
AAA_LORA = 96
GATE_LORA = 256
RWKV_COLS = 3 * RWKV_WIDTH + DECAY_LORA + AAA_LORA + GATE_LORA
GN_EPS = 64e-5
ATTN_HEADS = 16
ATTN_HEAD_DIM = 64
ATTN_WIDTH = ATTN_HEADS * ATTN_HEAD_DIM
IDX_HEADS = 16
IDX_DIM = 64
TOPK_MAX = 256
Q_BLOCK = 128
IDX_COLS = IDX_HEADS * IDX_DIM + IDX_DIM + IDX_HEADS
GATE_COLS = 2 * D_MODEL
IN_COLS = RWKV_COLS + 3 * ATTN_WIDTH + IDX_COLS + GATE_COLS
D_FF = -(-8 * D_MODEL // (3 * 256)) * 256
RMS_EPS = 1e-6

kernel_name = 'rwkv7_dsa_gated_hybrid_step'


def rms_norm(x, g):
    xf = x.astype(jnp.float32)
    y = xf * lax.rsqrt(jnp.mean(xf * xf, axis=-1, keepdims=True) + RMS_EPS)
    return (y * g.astype(jnp.float32)).astype(x.dtype)


def split_cols(p, sizes):
    offs, o = [], 0
    for s in sizes[:-1]:
        o += s
        offs.append(o)
    return jnp.split(p, offs, axis=-1)


def rwkv7_time_mix(z_cur, z_prev, wkv0, rwkv_mu, w0, w_w2, a0, w_a2, w_g2, k_k, k_a, r_k, ln_x_w, ln_x_b, w_o_rwkv):
    B, T, _ = z_cur.shape
    H, N = RWKV_HEADS, RWKV_HEAD_DIM
    dt = z_cur.dtype
    prev = jnp.concatenate([z_prev[:, None].astype(dt), z_cur[:, :-1]], axis=1)
    z = z_cur + (prev - z_cur) * rwkv_mu
    r, k, v, wd, ad, gd = split_cols(z, (RWKV_WIDTH, RWKV_WIDTH, RWKV_WIDTH, DECAY_LORA, AAA_LORA, GATE_LORA))
    w_log = -jax.nn.softplus(-(w0 + jnp.tanh(wd) @ w_w2).astype(jnp.float32)) - 0.5
    decay = jnp.exp(-jnp.exp(w_log)).astype(dt)
    a = jax.nn.sigmoid(a0 + ad @ w_a2)
    g = jax.nn.sigmoid(gd) @ w_g2
    hs = lambda t: t.reshape(B, T, H, N)
    r, k, v, decay, a = hs(r), hs(k), hs(v), hs(decay), hs(a)
    kk = (k * k_k.reshape(H, N)).astype(jnp.float32)
    kk = (kk / jnp.maximum(jnp.linalg.norm(kk, axis=-1, keepdims=True), 1e-12)).astype(dt)
    k = k * (1 + (a - 1) * k_a.reshape(H, N))

    def step(S, inp):
        r_t, w_t, k_t, v_t, kk_t, a_t = inp
        sa = jnp.einsum('bhvk,bhk->bhv', S, -kk_t)
        S = S * w_t[:, :, None, :] + sa[..., None] * (kk_t * a_t)[:, :, None, :] + v_t[..., None] * k_t[:, :, None, :]
        return S, jnp.einsum('bhvk,bhk->bhv', S, r_t)

    seq = tuple(jnp.swapaxes(t, 0, 1) for t in (r, decay, k, v, kk, a))
    wkv_fin, y = lax.scan(step, wkv0.astype(dt), seq)
    y = jnp.swapaxes(y, 0, 1).astype(jnp.float32)
    mu = jnp.mean(y, axis=-1, keepdims=True)
    var = jnp.mean(jnp.square(y - mu), axis=-1, keepdims=True)
    yn = ((y - mu) * lax.rsqrt(var + GN_EPS)).reshape(B, T, RWKV_WIDTH)
    yn = (yn * ln_x_w.astype(jnp.float32) + ln_x_b.astype(jnp.float32)).astype(dt)
    bonus = (jnp.sum(r * k * r_k, axis=-1, keepdims=True) * v).reshape(B, T, RWKV_WIDTH)
    out = ((yn + bonus) * g) @ w_o_rwkv
    return out, wkv_fin, z_cur[:, -1]


def index_scores(q_idx, w_idx, k_idx):
    s = jax.nn.relu(jnp.einsum('bthd,bsd->bths', q_idx, k_idx).astype(jnp.float32) * IDX_DIM ** -0.5)
    return jnp.einsum('bths,bth->bts', s, w_idx.astype(jnp.float32) * IDX_HEADS ** -0.5)


def sparse_attend(q, k_sel, v_sel, valid):
    s = jnp.einsum('bthd,btkhd->bthk', q, k_sel).astype(jnp.float32) * ATTN_HEAD_DIM ** -0.5
    s = jnp.where(valid[:, :, None, :], s, -jnp.inf)
    p = jax.nn.softmax(s, axis=-1).astype(v_sel.dtype)
    return jnp.einsum('bthk,btkhd->bthd', p, v_sel)


_gather_rows = jax.vmap(lambda rows, idx: rows[idx])


def dsa_prompt(q, k, v, q_idx, k_idx, w_idx):
    B, S = q.shape[:2]
    topk = min(TOPK_MAX, S // 4)
    nb = S // Q_BLOCK
    blk = lambda t: jnp.swapaxes(t.reshape(B, nb, Q_BLOCK, *t.shape[2:]), 0, 1)
    qpos = jnp.arange(S, dtype=jnp.int32).reshape(nb, Q_BLOCK)
    key_pos = jnp.arange(S, dtype=jnp.int32)

    def one_block(args):
        qb, qib, wb, pb = args
        sc = index_scores(qib, wb, k_idx)
        sc = jnp.where(key_pos[None, None, :] <= pb[None, :, None], sc, -jnp.inf)
        _, idx = lax.top_k(sc, topk)
        valid = idx <= pb[None, :, None]
        return sparse_attend(qb, _gather_rows(k, idx), _gather_rows(v, idx), valid)

    out = lax.map(one_block, (blk(q), blk(q_idx), blk(w_idx), qpos))
    return jnp.swapaxes(out, 0, 1).reshape(B, S, ATTN_WIDTH)


def dsa_sample(q, k_new, v_new, q_idx, kidx_new, w_idx, cache_k, cache_v, cache_kidx, page_table):
    DB, T = q.shape[:2]
    past = page_table.shape[1] * PAGE_SIZE
    L = past + T
    topk = min(TOPK_MAX, L // 4)
    kidx_past = cache_kidx[page_table].reshape(DB, past, IDX_DIM)
    kidx_all = jnp.concatenate([kidx_past, kidx_new.astype(kidx_past.dtype)], axis=1)
    qpos = past + jnp.arange(T, dtype=jnp.int32)
    key_pos = jnp.arange(L, dtype=jnp.int32)
    sc = index_scores(q_idx, w_idx, kidx_all)
    sc = jnp.where(key_pos[None, None, :] <= qpos[None, :, None], sc, -jnp.inf)
    _, idx = lax.top_k(sc, topk)
    valid = idx <= qpos[None, :, None]
    in_past = (idx < past)[..., None, None]
    pidx = jnp.minimum(idx, past - 1)
    page = jnp.take_along_axis(page_table, (pidx // PAGE_SIZE).reshape(DB, -1), axis=1).reshape(pidx.shape)
    phys = page * PAGE_SIZE + pidx % PAGE_SIZE
    nidx = jnp.clip(idx - past, 0, T - 1)
    flat_k = cache_k.reshape(-1, ATTN_HEADS, ATTN_HEAD_DIM)
    flat_v = cache_v.reshape(-1, ATTN_HEADS, ATTN_HEAD_DIM)
    k_sel = jnp.where(in_past, flat_k[phys], _gather_rows(k_new, nidx).astype(flat_k.dtype))
    v_sel = jnp.where(in_past, flat_v[phys], _gather_rows(v_new, nidx).astype(flat_v.dtype))
    return sparse_attend(q, k_sel, v_sel, valid).reshape(DB, T, ATTN_WIDTH)


def hybrid_layer(x, shift0, wkv0, attend, g_pre_mix, w_in, rwkv_mu, w0, w_w2, a0, w_a2, w_g2, k_k, k_a, r_k,
                 ln_x_w, ln_x_b, w_o_rwkv, w_o_attn, w_out, g_post_mix, g_pre_ffn, w_gate, w_up, w_down, g_post_ffn):
    B, T, _ = x.shape
    u = rms_norm(x, g_pre_mix)
    p = u @ w_in
    z_rwkv, q, k, v, q_idx, k_idx, w_idx, gate_a, gate_b = split_cols(
        p, (RWKV_COLS, ATTN_WIDTH, ATTN_WIDTH, ATTN_WIDTH, IDX_HEADS * IDX_DIM, IDX_DIM, IDX_HEADS, D_MODEL, D_MODEL))
    rwkv_y, wkv_new, shift_new = rwkv7_time_mix(z_rwkv, shift0, wkv0, rwkv_mu, w0, w_w2, a0, w_a2, w_g2,
                                                k_k, k_a, r_k, ln_x_w, ln_x_b, w_o_rwkv)
    q = q.reshape(B, T, ATTN_HEADS, ATTN_HEAD_DIM)
    k = k.reshape(B, T, ATTN_HEADS, ATTN_HEAD_DIM)
    v = v.reshape(B, T, ATTN_HEADS, ATTN_HEAD_DIM)
    q_idx = q_idx.reshape(B, T, IDX_HEADS, IDX_DIM)
    attn_y = attend(q, k, v, q_idx, k_idx, w_idx) @ w_o_attn
    m = jax.nn.sigmoid(gate_a) * rwkv_y + jax.nn.sigmoid(gate_b) * attn_y
    h = x + rms_norm(m @ w_out, g_post_mix)
    f = rms_norm(h, g_pre_ffn)
    f = (jax.nn.silu(f @ w_gate) * (f @ w_up)) @ w_down
    y = h + rms_norm(f, g_post_ffn)
    return y, k, v, k_idx, wkv_new, shift_new


def setup_inputs(seed: int = 0) -> dict:
    key = jax.random.key(seed)
    ks = jax.random.split(key, 32)
    n_pages = PAST_LEN // PAGE_SIZE
    n_pool = (DEC_BATCH * n_pages * 5) // 4
    nrm = lambda k, shape, s: jax.random.normal(k, shape, jnp.float32) * s
    gain = lambda k, n: 1.0 + nrm(k, (n,), 0.05)
    page_table = jax.random.permutation(ks[7], n_pool)[: DEC_BATCH * n_pages].reshape(DEC_BATCH, n_pages).astype(jnp.int32)
    return {
        'x_prompt': nrm(ks[0], (BATCH, SEQ, D_MODEL), 1.0),
        'x_sample': nrm(ks[1], (DEC_BATCH, DEC_SEQ, D_MODEL), 1.0),
        'cache_k': nrm(ks[2], (n_pool, PAGE_SIZE, ATTN_HEADS, ATTN_HEAD_DIM), 1.0),
        'cache_v': nrm(ks[3], (n_pool, PAGE_SIZE, ATTN_HEADS, ATTN_HEAD_DIM), 1.0),
        'cache_kidx': nrm(ks[4], (n_pool, PAGE_SIZE, IDX_DIM), 1.0),
        'state_wkv': nrm(ks[5], (DEC_BATCH, RWKV_HEADS, RWKV_HEAD_DIM, RWKV_HEAD_DIM), 0.2),
        'state_shift': nrm(ks[6], (DEC_BATCH, RWKV_COLS), 1.0),
        'page_table': page_table,
        'g_pre_mix': gain(ks[8], D_MODEL),
        'w_in': nrm(ks[9], (D_MODEL, IN_COLS), D_MODEL ** -0.5),
        'rwkv_mu': jax.random.uniform(ks[10], (RWKV_COLS,), jnp.float32),
        'w0': jax.random.uniform(ks[11], (RWKV_WIDTH,), jnp.float32, -5.0, 1.0),
        'w_w2': nrm(ks[12], (DECAY_LORA, RWKV_WIDTH), 0.1),
        'a0': nrm(ks[13], (RWKV_WIDTH,), 0.1),
        'w_a2': nrm(ks[14], (AAA_LORA, RWKV_WIDTH), 0.1),
        'w_g2': nrm(ks[15], (GATE_LORA, RWKV_WIDTH), GATE_LORA ** -0.5),
        'k_k': 0.85 + nrm(ks[16], (RWKV_WIDTH,), 0.05),
        'k_a': gain(ks[17], RWKV_WIDTH),
        'r_k': nrm(ks[18], (RWKV_HEADS, RWKV_HEAD_DIM), 0.1),
        'ln_x_w': gain(ks[19], RWKV_WIDTH),
        'ln_x_b': nrm(ks[20], (RWKV_WIDTH,), 0.01),
        'w_o_rwkv': nrm(ks[21], (RWKV_WIDTH, D_MODEL), RWKV_WIDTH ** -0.5),
        'w_o_attn': nrm(ks[22], (ATTN_WIDTH, D_MODEL), ATTN_WIDTH ** -0.5),
        'w_out': nrm(ks[23], (D_MODEL, D_MODEL), D_MODEL ** -0.5),
        'g_post_mix': gain(ks[24], D_MODEL),
        'g_pre_ffn': gain(ks[25], D_MODEL),
        'w_gate': nrm(ks[26], (D_MODEL, D_FF), D_MODEL ** -0.5),
        'w_up': nrm(ks[27], (D_MODEL, D_FF), D_MODEL ** -0.5),
        'w_down': nrm(ks[28], (D_FF, D_MODEL), D_FF ** -0.5),
        'g_post_ffn': gain(ks[29], D_MODEL),
    }


def reference(x_prompt, x_sample, cache_k, cache_v, cache_kidx, state_wkv, state_shift, page_table,
              g_pre_mix, w_in, rwkv_mu, w0, w_w2, a0, w_a2, w_g2, k_k, k_a, r_k, ln_x_w, ln_x_b,
              w_o_rwkv, w_o_attn, w_out, g_post_mix, g_pre_ffn, w_gate, w_up, w_down, g_post_ffn):
    weights = (g_pre_mix, w_in, rwkv_mu, w0, w_w2, a0, w_a2, w_g2, k_k, k_a, r_k, ln_x_w, ln_x_b,
               w_o_rwkv, w_o_attn, w_out, g_post_mix, g_pre_ffn, w_gate, w_up, w_down, g_post_ffn)
    B = x_prompt.shape[0]
    shift_zero = jnp.zeros((B, RWKV_COLS), x_prompt.dtype)
    wkv_zero = jnp.zeros((B, RWKV_HEADS, RWKV_HEAD_DIM, RWKV_HEAD_DIM), x_prompt.dtype)
    sample_attend = lambda q, k, v, qi, ki, wi: dsa_sample(q, k, v, qi, ki, wi, cache_k, cache_v, cache_kidx, page_table)

    y_prompt = x_prompt
    y_sample = x_sample
    for _ in range(DEPTH):
        y_prompt, k_p, v_p, kidx_p, wkv_p, shift_p = hybrid_layer(y_prompt, shift_zero, wkv_zero, dsa_prompt, *weights)
        y_sample, k_s, v_s, kidx_s, wkv_s, shift_s = hybrid_layer(y_sample, state_shift, state_wkv, sample_attend, *weights)
    return (y_prompt, y_sample, k_p, v_p, kidx_p, wkv_p, shift_p, k_s, v_s, kidx_s, wkv_s, shift_s)
```

```python
import functools

import numpy as np
import jax
import jax.numpy as jnp
from jax import lax
from jax.experimental import pallas as pl
from jax.experimental.pallas import tpu as pltpu

F32, BF16, I32 = jnp.float32, jnp.bfloat16, jnp.int32

D_MODEL = 2048
HEADS = 16
HEAD_DIM = 64
WIDTH = HEADS * HEAD_DIM
LORA_SMALL = 96
LORA_GATE = 256
RWKV_COLS = 3 * WIDTH + 2 * LORA_SMALL + LORA_GATE
IDX_DIM = 64
TOPK = 256
PAGE = 128
RMS_EPS = 1e-6
GN_EPS = 64e-5

LANES = 128
VMEM_BYTES = 64 * 1024 * 1024
PAIRS = WIDTH // LANES

C_R, C_K, C_V, C_AQ, C_AK, C_AV, C_GA, C_GB, C_LORA = 0, 1024, 2048, 3072, 4096, 5120, 6144, 8192, 10240
N_MAIN = 10752
N_IDX = 1152

INT_MIN = int(np.iinfo(np.int32).min)
NEG = -1e30

_NN = (((1,), (0,)), ((), ()))
_NT = (((1,), (1,)), ((), ()))


def _params(sem, vmem_mb):
    return pltpu.CompilerParams(dimension_semantics=sem, vmem_limit_bytes=vmem_mb * 1024 * 1024)


def _dot(a, b, dims=_NN):
    return lax.dot_general(a, b, dims, preferred_element_type=F32)


def _split(x):
    hi = x.astype(BF16)
    return hi, (x - hi.astype(F32)).astype(BF16)


def _mm3(ap, bp, dims=_NN):
    return _dot(ap[0], bp[0], dims) + (_dot(ap[0], bp[1], dims) + _dot(ap[1], bp[0], dims))


def _sigmoid(x):
    return 1.0 / (1.0 + jnp.exp(-x))


def _headsum(x, ones_bd):
    hi, lo = _split(x)
    return _dot(hi, ones_bd) + _dot(lo, ones_bd)


def _headsum_full(x, ones_bd):
    return jnp.concatenate(
        [_headsum(x[:, p * LANES:(p + 1) * LANES], ones_bd) for p in range(PAIRS)], axis=1)


def _rms_kernel(x_ref, g_ref, hi_ref, lo_ref):
    x = x_ref[...]
    y = x * lax.rsqrt(jnp.mean(x * x, axis=-1, keepdims=True) + RMS_EPS) * g_ref[...]
    hi = y.astype(BF16)
    hi_ref[...] = hi
    lo_ref[...] = (y - hi.astype(F32)).astype(BF16)


def _rms_call(x, g, tm):
    m, d = x.shape
    spec = pl.BlockSpec((tm, d), lambda i: (i, 0))
    return pl.pallas_call(
        _rms_kernel,
        out_shape=(jax.ShapeDtypeStruct((m, d), BF16), jax.ShapeDtypeStruct((m, d), BF16)),
        grid=(m // tm,),
        in_specs=[spec, pl.BlockSpec((1, d), lambda i: (0, 0))],
        out_specs=(spec, spec),
        compiler_params=_params(("arbitrary",), 32),
        name="rms_norm",
    )(x, g)


def _mm_kernel(*refs, na, nb, terms):
    a, b, o = refs[:na], refs[na:na + nb], refs[na + nb]
    acc = None
    for ia, ib in terms:
        d = _dot(a[ia][...], b[ib][...])
        acc = d if acc is None else acc + d
    o[...] = acc.astype(o.dtype)


def _mm_call(a_list, b_list, terms, out_dtype, tm, tn, name):
    m, k = a_list[0].shape
    n = b_list[0].shape[1]
    a_spec = pl.BlockSpec((tm, k), lambda j, i: (i, 0))
    b_spec = pl.BlockSpec((k, tn), lambda j, i: (0, j))
    return pl.pallas_call(
        functools.partial(_mm_kernel, na=len(a_list), nb=len(b_list), terms=terms),
        out_shape=jax.ShapeDtypeStruct((m, n), out_dtype),
        grid=(n // tn, m // tm),
        in_specs=[a_spec] * len(a_list) + [b_spec] * len(b_list),
        out_specs=pl.BlockSpec((tm, tn), lambda j, i: (i, j)),
        compiler_params=_params(("arbitrary", "arbitrary"), 48),
        name=name,
    )(*a_list, *b_list)


def _shift_rows(z, first_row):
    rows = lax.broadcasted_iota(I32, z.shape, 0)
    return jnp.where(rows == 0, first_row, pltpu.roll(z, 1, 0))


def _prep_kernel(rkv_ref, lora_ref, p_rkv_ref, p_lora_ref, mu_rkv_ref, mu_lora_ref, w0_ref, a0_ref,
                 kk_ref, ka_ref, rk_ref, ww2_ref, wa2_ref, wg2_ref, ones_ref,
                 logw_ref, kap_ref, bb_ref, kt_ref, r_ref, v_ref, bonus_ref, g_ref,
                 c_rkv, c_lora, *, carry):
    z = rkv_ref[...]
    zl = lora_ref[...]
    if carry:
        @pl.when(pl.program_id(1) == 0)
        def _():
            c_rkv[...] = p_rkv_ref[0]
            c_lora[...] = p_lora_ref[0]
        zp = _shift_rows(z, c_rkv[...])
        zlp = _shift_rows(zl, c_lora[...])
        tt = z.shape[0]
        c_rkv[...] = z[tt - 1:tt, :]
        c_lora[...] = zl[tt - 1:tt, :]
    else:
        zp = p_rkv_ref[...]
        zlp = p_lora_ref[...]
    zs = z + (zp - z) * mu_rkv_ref[...]
    zls = zl + (zlp - zl) * mu_lora_ref[...]
    r, k, v = zs[:, 0:WIDTH], zs[:, WIDTH:2 * WIDTH], zs[:, 2 * WIDTH:3 * WIDTH]
    wd, ad, gd = zls[:, 0:LANES], zls[:, LANES:2 * LANES], zls[:, 2 * LANES:4 * LANES]
    ones_bd = ones_ref[...]

    lw = w0_ref[...] + _dot(jnp.tanh(wd).astype(BF16), ww2_ref[...])
    nlw = -lw
    softplus = jnp.maximum(nlw, 0.0) + jnp.log(1.0 + jnp.exp(-jnp.abs(nlw)))
    logw_ref[...] = -jnp.exp(-softplus - 0.5)
    a = _sigmoid(a0_ref[...] + _dot(ad.astype(BF16), wa2_ref[...]))
    g_ref[...] = _dot(_sigmoid(gd).astype(BF16), wg2_ref[...])
    kkv = k * kk_ref[...]
    norm = jnp.sqrt(_headsum_full(kkv * kkv, ones_bd))
    kap = kkv / jnp.maximum(norm, 1e-12)
    kt = k * (1.0 + (a - 1.0) * ka_ref[...])
    kap_ref[...] = kap
    bb_ref[...] = kap * a
    kt_ref[...] = kt
    r_ref[...] = r
    v_ref[...] = v
    bonus_ref[...] = _headsum_full(r * kt * rk_ref[...], ones_bd) * v


def _prep_call(p_main, prev_rkv, prev_lora, wts, nb, t, tt, carry):
    m = nb * t
    nt = t // tt
    tok = lambda n, cb: pl.BlockSpec((tt, n), lambda b, i: (b * nt + i, cb))
    if carry:
        prev_specs = [pl.BlockSpec((1, 1, 3 * WIDTH), lambda b, i: (b, 0, 0)),
                      pl.BlockSpec((1, 1, 4 * LANES), lambda b, i: (b, 0, 0))]
    else:
        prev_specs = [tok(3 * WIDTH, 0), tok(4 * LANES, 0)]
    full = lambda a: pl.BlockSpec(a.shape, lambda b, i: (0,) * a.ndim)
    out_spec = tok(WIDTH, 0)
    return pl.pallas_call(
        functools.partial(_prep_kernel, carry=carry),
        out_shape=tuple(jax.ShapeDtypeStruct((m, WIDTH), F32) for _ in range(8)),
        grid=(nb, nt),
        in_specs=[tok(3 * WIDTH, 0), tok(4 * LANES, C_LORA // (4 * LANES))] + prev_specs + [full(a) for a in wts],
        out_specs=(out_spec,) * 8,
        scratch_shapes=[pltpu.VMEM((1, 3 * WIDTH), F32), pltpu.VMEM((1, 4 * LANES), F32)],
        compiler_params=_params(("arbitrary", "arbitrary"), 48),
        name="rwkv_prep",
    )(p_main, p_main, prev_rkv, prev_lora, *wts)


def _post_math(y, bonus, g, lnw, lnb, ones_bd):
    mu = _headsum(y, ones_bd) * (1.0 / HEAD_DIM)
    d = y - mu
    var = _headsum(d * d, ones_bd) * (1.0 / HEAD_DIM)
    yn = d * lax.rsqrt(var + GN_EPS) * lnw + lnb
    return ((yn + bonus) * g).astype(BF16)


def _post_kernel(y_ref, bonus_ref, g_ref, lnw_ref, lnb_ref, ones_ref, o_ref):
    for p in range(PAIRS):
        sl = slice(p * LANES, (p + 1) * LANES)
        o_ref[:, sl] = _post_math(y_ref[:, sl], bonus_ref[:, sl], g_ref[:, sl], lnw_ref[:, sl],
                                  lnb_ref[:, sl], ones_ref[...])


def _post_call(y, bonus, g, lnw, lnb, ones_bd):
    return pl.pallas_call(
        _post_kernel,
        out_shape=jax.ShapeDtypeStruct(y.shape, BF16),
        name="rwkv_post",
    )(y, bonus, g, lnw, lnb, ones_bd)


CHUNK = 64


def _chunk_kernel(logw_ref, kap_ref, bb_ref, kt_ref, r_ref, v_ref, bonus_ref, g_ref, lnw_ref, lnb_ref,
                  ones_ref, tri_ref, rw_ref, sfin_ref, s_ref):
    c = pl.program_id(1)

    @pl.when(c == 0)
    def _():
        s_ref[...] = jnp.zeros_like(s_ref)

    lane = lax.broadcasted_iota(I32, (1, LANES), 1)
    m0 = (lane < HEAD_DIM).astype(F32)
    m1 = 1.0 - m0
    rr = lax.broadcasted_iota(I32, (LANES, LANES), 0)
    cc = lax.broadcasted_iota(I32, (LANES, LANES), 1)
    strict = (cc & (CHUNK - 1)) < (rr & (CHUNK - 1))
    incl = (cc & (CHUNK - 1)) <= (rr & (CHUNK - 1))
    eye = (rr == cc).astype(F32)
    ones_bd = ones_ref[...]
    tri = tri_ref[...]

    def stack(x):
        return jnp.concatenate([x * m0, x * m1], axis=0)

    for p in range(PAIRS):
        sl = slice(p * LANES, (p + 1) * LANES)
        lw = logw_ref[:, sl]
        h1 = lw.astype(BF16)
        r1 = lw - h1.astype(F32)
        h2 = r1.astype(BF16)
        h3 = (r1 - h2.astype(F32)).astype(BF16)
        lam = _dot(tri, h1) + (_dot(tri, h2) + _dot(tri, h3))
        lam_c = lam[CHUNK - 1:CHUNK, :]
        e_in = jnp.exp(lam)
        e_ex = jnp.exp(lam - lw)
        e_inv = jnp.exp(-lam)
        e_rem = jnp.exp(lam_c - lam)
        kap, bb, kt = kap_ref[:, sl], bb_ref[:, sl], kt_ref[:, sl]
        s_kh = _split(stack(kap * e_ex))
        rh = stack(r_ref[:, sl] * e_in)
        s_rh = _split(rh)
        s_bc = _split(stack(bb * e_inv))
        s_kc = _split(stack(kt * e_inv))
        s_bt_t = _split(stack(bb * e_rem).T)
        s_kt_t = _split(stack(kt * e_rem).T)
        s_v = _split(stack(v_ref[:, sl]))

        l_b = jnp.where(strict, _mm3(s_kh, s_bc, _NT), 0.0)
        l_k = jnp.where(strict, _mm3(s_kh, s_kc, _NT), 0.0)
        a_b = _split(jnp.where(incl, _mm3(s_rh, s_bc, _NT), 0.0))
        a_k = _split(jnp.where(incl, _mm3(s_rh, s_kc, _NT), 0.0))

        pw = -l_b
        tinv = eye + pw
        for _ in range(5):
            pws = _split(pw)
            pw = _mm3(pws, pws)
            tinv = tinv + _mm3(_split(tinv), _split(pw))
        tinv = _split(tinv)

        s_wm = _split(_mm3(tinv, s_kh))
        s_uv = _split(_mm3(tinv, _split(_mm3(_split(l_k), s_v))))
        m_mat = eye * jnp.exp(lam_c) - _mm3(s_bt_t, s_wm)
        n_mat = _mm3(s_kt_t, s_v) - _mm3(s_bt_t, s_uv)
        s_ry = rh - _mm3(a_b, s_wm)
        s_y0 = _mm3(a_k, s_v) - _mm3(a_b, s_uv)

        s0 = _split(s_ref[p])
        s_y = _mm3(_split(s_ry), s0) + s_y0
        y = s_y[0:CHUNK, :] + s_y[CHUNK:2 * CHUNK, :]
        s_ref[p] = _mm3(_split(m_mat), s0) + n_mat

        rw_ref[:, sl] = _post_math(y, bonus_ref[:, sl], g_ref[:, sl], lnw_ref[:, sl], lnb_ref[:, sl], ones_bd)

    @pl.when(c == pl.num_programs(1) - 1)
    def _():
        sfin_ref[0] = s_ref[...]


def _chunk_call(feats, lnw, lnb, ones_bd, tri, nb, t):
    nc = t // CHUNK
    tok = pl.BlockSpec((CHUNK, WIDTH), lambda b, c: (b * nc + c, 0))
    full = lambda a: pl.BlockSpec(a.shape, lambda b, c: (0,) * a.ndim)
    return pl.pallas_call(
        _chunk_kernel,
        out_shape=(jax.ShapeDtypeStruct((nb * t, WIDTH), BF16),
                   jax.ShapeDtypeStruct((nb, PAIRS, LANES, LANES), F32)),
        grid=(nb, nc),
        in_specs=[tok] * 8 + [full(lnw), full(lnb), full(ones_bd), full(tri)],
        out_specs=(tok, pl.BlockSpec((1, PAIRS, LANES, LANES), lambda b, c: (b, 0, 0, 0))),
        scratch_shapes=[pltpu.VMEM((PAIRS, LANES, LANES), F32)],
        compiler_params=_params(("arbitrary", "arbitrary"), 32),
        name="rwkv_chunk_scan",
    )(*feats, lnw, lnb, ones_bd, tri)


def _step_kernel(s_ref, logw_ref, kap_ref, bb_ref, kt_ref, r_ref, v_ref, so_ref, y_ref):
    s = s_ref[0]
    sa = -jnp.sum(s * kap_ref[0], axis=-1, keepdims=True)
    s_new = s * jnp.exp(logw_ref[0]) + sa * bb_ref[0] + v_ref[0] * kt_ref[0]
    so_ref[0] = s_new
    y_ref[0] = jnp.sum(s_new * r_ref[0], axis=-1, keepdims=True)


def _step_call(state, logw, kap, bb, kt, r, v):
    nb = state.shape[0]
    row = lambda x: x.reshape(nb, HEADS, 1, HEAD_DIM)
    s_spec = pl.BlockSpec((1, HEADS, HEAD_DIM, HEAD_DIM), lambda b: (b, 0, 0, 0))
    r_spec = pl.BlockSpec((1, HEADS, 1, HEAD_DIM), lambda b: (b, 0, 0, 0))
    c_spec = pl.BlockSpec((1, HEADS, HEAD_DIM, 1), lambda b: (b, 0, 0, 0))
    return pl.pallas_call(
        _step_kernel,
        out_shape=(jax.ShapeDtypeStruct(state.shape, F32), jax.ShapeDtypeStruct((nb, HEADS, HEAD_DIM, 1), F32)),
        grid=(nb,),
        in_specs=[s_spec] + [r_spec] * 5 + [c_spec],
        out_specs=(s_spec, c_spec),
        compiler_params=_params(("arbitrary",), 32),
        name="rwkv_step",
    )(state, row(logw), row(kap), row(bb), row(kt), row(r), v.reshape(nb, HEADS, HEAD_DIM, 1))


def _sortable(x):
    x = jnp.where(x == 0.0, 0.0, x)
    bits = pltpu.bitcast(x, I32)
    return jnp.where(bits < 0, bits ^ jnp.int32(0x7FFFFFFF), bits)


def _select_topk(keys_ref, nkc, rows, kc_size, col_of, idx_bits):
    def count(pred):
        def body(kc, acc):
            x = jnp.where(pred(kc), 1.0, 0.0)
            part = x[:, 0:LANES]
            for q in range(1, kc_size // LANES):
                part = part + x[:, q * LANES:(q + 1) * LANES]
            return acc + part
        acc = lax.fori_loop(0, nkc, body, jnp.zeros((rows, LANES), F32))
        return jnp.sum(acc, axis=-1, keepdims=True)

    kf = float(TOPK)
    thr = jnp.where(count(lambda kc: keys_ref[kc] >= 0) >= kf, jnp.int32(0), jnp.int32(INT_MIN))

    def bit_body(it, thr):
        cand = thr + lax.shift_left(jnp.int32(1), jnp.int32(30) - it)
        return jnp.where(count(lambda kc: keys_ref[kc] >= cand) >= kf, cand, thr)

    thr = lax.fori_loop(0, 31, bit_body, thr)
    need = kf - count(lambda kc: keys_ref[kc] > thr)

    def tie_body(it, x):
        cand = x + lax.shift_left(jnp.int32(1), jnp.int32(idx_bits - 1) - it)
        c = count(lambda kc: (keys_ref[kc] == thr) & (col_of(kc) <= cand))
        return jnp.where(c < need, cand, x)

    x = lax.fori_loop(0, idx_bits, tie_body, jnp.full((rows, 1), -1, I32))
    return thr, x + 1


TQ = 128
KC = 512


def _attn_kernel(qidx_ref, widx_ref, kz0h_ref, kz0l_ref, kz1h_ref, kz1l_ref, aq_ref, kb_ref, vb_ref,
                 att_ref, keys_ref, bias_ref):
    i = pl.program_id(1)
    nkc = lax.shift_right_logical((i + 1) * TQ + (KC - 1), int(np.log2(KC)))
    row = i * TQ + lax.broadcasted_iota(I32, (TQ, KC), 0)
    lane_kc = lax.broadcasted_iota(I32, (TQ, KC), 1)
    col_of = lambda kc: kc * KC + lane_kc

    qh, ql = _split(qidx_ref[...])
    wv = widx_ref[:, IDX_DIM:IDX_DIM + HEADS] * (HEADS ** -0.5 * IDX_DIM ** -0.5)
    kz = ((kz0h_ref, kz0l_ref), (kz1h_ref, kz1l_ref))

    def score_body(kc, carry):
        off = pl.multiple_of(kc * KC, KC)
        acc = jnp.zeros((TQ, KC), F32)
        for p in range(PAIRS):
            sl = slice(p * LANES, (p + 1) * LANES)
            for hh in range(2):
                kh = kz[hh][0][pl.ds(off, KC), :]
                kl = kz[hh][1][pl.ds(off, KC), :]
                d = _dot(qh[:, sl], kh, _NT) + (_dot(qh[:, sl], kl, _NT) + _dot(ql[:, sl], kh, _NT))
                h = 2 * p + hh
                acc = acc + jnp.maximum(d, 0.0) * wv[:, h:h + 1]
        keys_ref[kc] = jnp.where(col_of(kc) <= row, _sortable(acc), jnp.int32(INT_MIN))
        return carry

    lax.fori_loop(0, nkc, score_body, 0)

    thr, jb = _select_topk(keys_ref, nkc, TQ, KC, col_of, 12)

    def bias_body(kc, carry):
        key = keys_ref[kc]
        col = col_of(kc)
        sel = ((key > thr) | ((key == thr) & (col <= jb))) & (col <= row)
        bias_ref[kc] = jnp.where(sel, 0.0, NEG)
        return carry

    lax.fori_loop(0, nkc, bias_body, 0)

    lane = lax.broadcasted_iota(I32, (1, LANES), 1)
    first = lane < HEAD_DIM
    for p in range(PAIRS):
        sl = slice(p * LANES, (p + 1) * LANES)
        aq = aq_ref[:, sl]
        outs = []
        for hh in range(2):
            qm = jnp.where(first if hh == 0 else ~first, aq, 0.0).astype(BF16)

            def body(kc, carry):
                m, l, acc = carry
                off = pl.multiple_of(kc * KC, KC)
                on = bias_ref[kc] == 0.0
                s = _dot(qm, kb_ref[pl.ds(off, KC), sl], _NT) * (HEAD_DIM ** -0.5)
                m_new = jnp.maximum(m, jnp.max(jnp.where(on, s, NEG), axis=-1, keepdims=True))
                e = jnp.where(on, jnp.exp(s - m_new), 0.0)
                alpha = jnp.exp(m - m_new)
                l = alpha * l + jnp.sum(e, axis=-1, keepdims=True)
                acc = alpha * acc + _dot(e.astype(BF16), vb_ref[pl.ds(off, KC), sl])
                return m_new, l, acc

            init = (jnp.full((TQ, 1), NEG, F32), jnp.zeros((TQ, 1), F32), jnp.zeros((TQ, LANES), F32))
            _, l, acc = lax.fori_loop(0, nkc, body, init)
            outs.append(acc / l)
        att_ref[:, sl] = jnp.where(first, outs[0], outs[1]).astype(BF16)


def _attn_call(p_idx, kz, p_main, kb, vb, nb, t):
    nq = t // TQ
    nkc = t // KC
    qtok = lambda n, cb: pl.BlockSpec((TQ, n), lambda b, i: (b * nq + i, cb))
    seq = lambda n: pl.BlockSpec((t, n), lambda b, i: (b, 0))
    return pl.pallas_call(
        _attn_kernel,
        out_shape=jax.ShapeDtypeStruct((nb * t, WIDTH), BF16),
        grid=(nb, nq),
        in_specs=[qtok(WIDTH, 0), qtok(LANES, WIDTH // LANES)] + [seq(LANES)] * 4
        + [qtok(WIDTH, C_AQ // WIDTH), seq(WIDTH), seq(WIDTH)],
        out_specs=qtok(WIDTH, 0),
        scratch_shapes=[pltpu.VMEM((nkc, TQ, KC), I32), pltpu.VMEM((nkc, TQ, KC), F32)],
        compiler_params=_params(("arbitrary", "arbitrary"), 52),
        name="dsa_prompt_attention",
    )(p_idx, p_idx, *kz, p_main, kb, vb)


PG = 8
SROWS = 72
SCOLS = SROWS * PAGE


def _sidx_kernel(pt_ref, q_ref, w_ref, knew_ref, *rest, npages):
    pages, o_ref = rest[:PG], rest[PG]
    j = pl.program_id(1)
    qh, ql = _split(q_ref[0])
    w = w_ref[0] * (HEADS ** -0.5 * IDX_DIM ** -0.5)

    def score(kpage):
        kh, kl = _split(kpage)
        d = _dot(qh, kh, _NT) + (_dot(qh, kl, _NT) + _dot(ql, kh, _NT))
        return jnp.sum(jnp.maximum(d, 0.0) * w, axis=0, keepdims=True)

    @pl.when(j == 0)
    def _():
        o_ref[0] = jnp.zeros((SROWS, PAGE), F32)
        o_ref[0, npages:npages + 1, :] = score(knew_ref[0])

    for g in range(PG):
        o_ref[0, pl.ds(j * PG + g, 1), :] = score(pages[g][0])


def _sidx_call(page_table, q, w, knew, cache_kidx):
    nb, npages = page_table.shape
    one = lambda s: pl.BlockSpec((1,) + s, lambda b, j, pt: (b, 0, 0))
    page = lambda g: pl.BlockSpec((1, PAGE, IDX_DIM), lambda b, j, pt: (pt[b, j * PG + g], 0, 0))
    return pl.pallas_call(
        functools.partial(_sidx_kernel, npages=npages),
        out_shape=jax.ShapeDtypeStruct((nb, SROWS, PAGE), F32),
        grid_spec=pltpu.PrefetchScalarGridSpec(
            num_scalar_prefetch=1,
            grid=(nb, npages // PG),
            in_specs=[one((HEADS, IDX_DIM)), one((HEADS, 1)), one((PAGE, IDX_DIM))] + [page(g) for g in range(PG)],
            out_specs=one((SROWS, PAGE)),
        ),
        compiler_params=_params(("arbitrary", "arbitrary"), 32),
        name="dsa_sample_index_scores",
    )(page_table, q, w, knew, *([cache_kidx] * PG))


SKC = 512
SNKC = SCOLS // SKC


def _ssel_kernel(sc_ref, bias_ref, keys_ref, *, n_valid):
    rows = sc_ref.shape[0]
    lane_kc = lax.broadcasted_iota(I32, (rows, SKC), 1)
    col_of = lambda kc: kc * SKC + lane_kc
    for c in range(SNKC):
        x = sc_ref[:, c * SKC:(c + 1) * SKC]
        keys_ref[c] = jnp.where(col_of(c) < n_valid, _sortable(x), jnp.int32(INT_MIN))
    thr, jb = _select_topk(keys_ref, SNKC, rows, SKC, col_of, 14)
    for c in range(SNKC):
        key = keys_ref[c]
        col = col_of(c)
        sel = ((key > thr) | ((key == thr) & (col <= jb))) & (col < n_valid)
        bias_ref[:, c * SKC:(c + 1) * SKC] = jnp.where(sel, 0.0, NEG)


def _ssel_call(scores, n_valid):
    rows = scores.shape[0]
    return pl.pallas_call(
        functools.partial(_ssel_kernel, n_valid=n_valid),
        out_shape=jax.ShapeDtypeStruct(scores.shape, F32),
        scratch_shapes=[pltpu.VMEM((SNKC, rows, SKC), I32)],
        name="dsa_sample_select",
    )(scores)


def _sattn_kernel(pt_ref, q_ref, kn_ref, vn_ref, bias_ref, exp_ref, *rest, npages):
    kpages, vpages = rest[:PG], rest[PG:2 * PG]
    o_ref, m_ref, l_ref, acc_ref = rest[2 * PG:]
    j = pl.program_id(1)
    scale = HEAD_DIM ** -0.5
    q = q_ref[0].astype(BF16)

    @pl.when(j == 0)
    def _():
        qf = q.astype(F32)
        s_new = jnp.sum(qf * kn_ref[0].astype(BF16).astype(F32), axis=-1, keepdims=True) * scale
        on = bias_ref[0, npages:npages + 1, 0:1] == 0.0
        m_ref[...] = jnp.where(on, s_new, NEG)
        l_ref[...] = jnp.where(on, 1.0, 0.0) * jnp.ones((HEADS, 1), F32)
        acc_ref[...] = jnp.where(on, vn_ref[0].astype(BF16).astype(F32), 0.0)

    rows = PAGE * HEADS
    diag = (lax.broadcasted_iota(I32, (HEADS, rows), 1) & (HEADS - 1)) == lax.broadcasted_iota(I32, (HEADS, rows), 0)
    for g in range(PG):
        brow = bias_ref[0, pl.ds(j * PG + g, 1), :]
        sel = jnp.broadcast_to(jnp.where(brow == 0.0, 1.0, 0.0), (HEADS, PAGE)).astype(BF16)
        on = diag & (_dot(sel, exp_ref[...]) > 0.5)
        s = _dot(q, kpages[g][...].astype(BF16), _NT) * scale
        m = m_ref[...]
        m_new = jnp.maximum(m, jnp.max(jnp.where(on, s, NEG), axis=-1, keepdims=True))
        e = jnp.where(on, jnp.exp(s - m_new), 0.0)
        alpha = jnp.exp(m - m_new)
        m_ref[...] = m_new
        l_ref[...] = alpha * l_ref[...] + jnp.sum(e, axis=-1, keepdims=True)
        acc_ref[...] = alpha * acc_ref[...] + _dot(e.astype(BF16), vpages[g][...].astype(BF16))

    @pl.when(j == pl.num_programs(1) - 1)
    def _():
        o_ref[0] = acc_ref[...] / l_ref[...]


def _sattn_call(page_table, q, kn, vn, bias, expand, ck_rows, cv_rows):
    nb, npages = page_table.shape
    rows = PAGE * HEADS
    one = lambda s: pl.BlockSpec((1,) + s, lambda b, j, pt: (b, 0, 0))
    page = lambda g: pl.BlockSpec((rows, HEAD_DIM), lambda b, j, pt: (pt[b, j * PG + g], 0))
    return pl.pallas_call(
        functools.partial(_sattn_kernel, npages=npages),
        out_shape=jax.ShapeDtypeStruct((nb, HEADS, HEAD_DIM), F32),
        grid_spec=pltpu.PrefetchScalarGridSpec(
            num_scalar_prefetch=1,
            grid=(nb, npages // PG),
            in_specs=[one((HEADS, HEAD_DIM))] * 3 + [one((SROWS, PAGE)),
                      pl.BlockSpec(expand.shape, lambda b, j, pt: (0, 0))]
            + [page(g) for g in range(PG)] * 2,
            out_specs=one((HEADS, HEAD_DIM)),
            scratch_shapes=[pltpu.VMEM((HEADS, 1), F32), pltpu.VMEM((HEADS, 1), F32),
                            pltpu.VMEM((HEADS, HEAD_DIM), F32)],
        ),
        compiler_params=_params(("arbitrary", "arbitrary"), 52),
        name="dsa_sample_attention",
    )(page_table, q, kn, vn, bias, expand, *([ck_rows] * PG), *([cv_rows] * PG))


def _merge_kernel(rw_ref, att_ref, wr_ref, wa_ref, ga_ref, gb_ref, o_ref):
    yr = _dot(rw_ref[...], wr_ref[...])
    ya = _dot(att_ref[...], wa_ref[...])
    o_ref[...] = (_sigmoid(ga_ref[...]) * yr + _sigmoid(gb_ref[...]) * ya).astype(BF16)


def _merge_call(rw, att, w_r, w_a, p_main, tm, tn):
    m = rw.shape[0]
    lhs = pl.BlockSpec((tm, WIDTH), lambda j, i: (i, 0))
    rhs = pl.BlockSpec((WIDTH, tn), lambda j, i: (0, j))
    gate = lambda c0: pl.BlockSpec((tm, tn), lambda j, i: (i, c0 // tn + j))
    return pl.pallas_call(
        _merge_kernel,
        out_shape=jax.ShapeDtypeStruct((m, D_MODEL), BF16),
        grid=(D_MODEL // tn, m // tm),
        in_specs=[lhs, lhs, rhs, rhs, gate(C_GA), gate(C_GB)],
        out_specs=pl.BlockSpec((tm, tn), lambda j, i: (i, j)),
        compiler_params=_params(("arbitrary", "arbitrary"), 32),
        name="mixer_merge",
    )(rw, att, w_r, w_a, p_main, p_main)


def _rms(x, g):
    return x * lax.rsqrt(jnp.mean(x * x, axis=-1, keepdims=True) + RMS_EPS) * g


def _outproj_kernel(m_ref, w_ref, x_ref, gpost_ref, gpre_ref, h_ref, f_ref):
    h = x_ref[...] + _rms(_dot(m_ref[...], w_ref[...]), gpost_ref[...])
    h_ref[...] = h
    f_ref[...] = _rms(h, gpre_ref[...]).astype(BF16)


def _outproj_call(mix, w_out, x, g_post, g_pre, tm):
    m = x.shape[0]
    tok = pl.BlockSpec((tm, D_MODEL), lambda i: (i, 0))
    full = lambda a: pl.BlockSpec(a.shape, lambda i: (0, 0))
    return pl.pallas_call(
        _outproj_kernel,
        out_shape=(jax.ShapeDtypeStruct((m, D_MODEL), F32), jax.ShapeDtypeStruct((m, D_MODEL), BF16)),
        grid=(m // tm,),
        in_specs=[tok, full(w_out), tok, full(g_post), full(g_pre)],
        out_specs=(tok, tok),
        compiler_params=_params(("arbitrary",), 40),
        name="mixer_out_proj",
    )(mix, w_out, x, g_post, g_pre)


def _ffn_kernel(f_ref, wg_ref, wu_ref, wd_ref, h_ref, gp_ref, y_ref, acc_ref):
    j = pl.program_id(1)
    f = f_ref[...]
    gate = _dot(f, wg_ref[...])
    t = (gate * _sigmoid(gate) * _dot(f, wu_ref[...])).astype(BF16)
    part = _dot(t, wd_ref[...])

    @pl.when(j == 0)
    def _():
        acc_ref[...] = part

    @pl.when(j > 0)
    def _():
        acc_ref[...] += part

    @pl.when(j == pl.num_programs(1) - 1)
    def _():
        y_ref[...] = h_ref[...] + _rms(acc_ref[...], gp_ref[...])


def _ffn_call(f, w_gate, w_up, w_down, h, g_post, tm, tf):
    m = f.shape[0]
    d_ff = w_gate.shape[1]
    tok = pl.BlockSpec((tm, D_MODEL), lambda i, j: (i, 0))
    return pl.pallas_call(
        _ffn_kernel,
        out_shape=jax.ShapeDtypeStruct((m, D_MODEL), F32),
        grid=(m // tm, d_ff // tf),
        in_specs=[tok, pl.BlockSpec((D_MODEL, tf), lambda i, j: (0, j)), pl.BlockSpec((D_MODEL, tf), lambda i, j: (0, j)),
                  pl.BlockSpec((tf, D_MODEL), lambda i, j: (j, 0)), tok, pl.BlockSpec((1, D_MODEL), lambda i, j: (0, 0))],
        out_specs=tok,
        scratch_shapes=[pltpu.VMEM((tm, D_MODEL), F32)],
        compiler_params=_params(("arbitrary", "arbitrary"), 48),
        name="swiglu_ffn",
    )(f, w_gate, w_up, w_down, h, g_post)


def _pad_cols(a, n):
    return jnp.pad(a, ((0, 0), (0, n - a.shape[1])))


def _regroup_cols(a):
    w3, ls = 3 * WIDTH, LORA_SMALL
    lora = jnp.concatenate([_pad_cols(a[:, w3:w3 + ls], LANES), _pad_cols(a[:, w3 + ls:w3 + 2 * ls], LANES),
                            a[:, w3 + 2 * ls:]], axis=1)
    return a[:, :w3], lora


def _ungroup_cols(rkv, lora):
    ls = LORA_SMALL
    return jnp.concatenate([rkv, lora[:, :ls], lora[:, LANES:LANES + ls], lora[:, 2 * LANES:]], axis=1)


def kernel(x_prompt, x_sample, cache_k, cache_v, cache_kidx, state_wkv, state_shift, page_table, g_pre_mix, w_in,
           rwkv_mu, w0, w_w2, a0, w_a2, w_g2, k_k, k_a, r_k, ln_x_w, ln_x_b, w_o_rwkv, w_o_attn, w_out,
           g_post_mix, g_pre_ffn, w_gate, w_up, w_down, g_post_ffn):
    nb, t, _ = x_prompt.shape
    ns = x_sample.shape[0]
    row = lambda a: a.reshape(1, -1).astype(F32)

    o_q = RWKV_COLS
    o_qi = o_q + 3 * WIDTH
    o_ki = o_qi + WIDTH
    o_ga = o_ki + IDX_DIM + HEADS
    w_rkv, w_lora = _regroup_cols(w_in[:, :RWKV_COLS])
    w_main = jnp.concatenate([w_rkv, w_in[:, o_q:o_qi], w_in[:, o_ga:], w_lora], axis=1).astype(BF16)
    w_idx_hi, w_idx_lo = _split(jnp.concatenate([w_in[:, o_qi:o_ki], _pad_cols(w_in[:, o_ki:o_ga], LANES)], axis=1))
    mu_rkv, mu_lora = _regroup_cols(row(rwkv_mu))
    pad_rows = lambda a: jnp.pad(a, ((0, LANES - a.shape[0]), (0, 0))).astype(BF16)
    lane = np.arange(LANES)
    ones_bd = jnp.asarray((lane[:, None] // HEAD_DIM) == (lane[None, :] // HEAD_DIM), BF16)
    tri = jnp.asarray(np.arange(CHUNK)[:, None] >= np.arange(CHUNK)[None, :], BF16)
    prep_w = [mu_rkv, mu_lora, row(w0), row(a0), row(k_k), row(k_a), row(r_k), pad_rows(w_w2), pad_rows(w_a2),
              w_g2.astype(BF16), ones_bd]
    lnw, lnb = row(ln_x_w), row(ln_x_b)
    w_or, w_oa, w_o = w_o_rwkv.astype(BF16), w_o_attn.astype(BF16), w_out.astype(BF16)
    w_g, w_u, w_d = w_gate.astype(BF16), w_up.astype(BF16), w_down.astype(BF16)
    gains = [row(g_pre_mix), row(g_post_mix), row(g_pre_ffn), row(g_post_ffn)]

    def project(x2d, tm):
        u_hi, u_lo = _rms_call(x2d, gains[0], min(tm, 512))
        p_main = _mm_call([u_hi], [w_main], [(0, 0)], F32, tm, 512, "in_proj")
        p_idx = _mm_call([u_hi, u_lo], [w_idx_hi, w_idx_lo], [(0, 0), (0, 1), (1, 0)], F32, tm, 384, "in_proj_idx")
        return p_main, p_idx

    def finish(x2d, p_main, rw, att, tm, tn, tm_ffn):
        mix = _merge_call(rw, att, w_or, w_oa, p_main, tm, tn)
        h, f = _outproj_call(mix, w_o, x2d, gains[1], gains[2], min(tm, 256))
        return _ffn_call(f, w_g, w_u, w_d, h, gains[3], tm_ffn, 512)

    m = nb * t
    xp = x_prompt.reshape(m, D_MODEL)
    p_main, p_idx = project(xp, 1024)
    zero_rkv = jnp.zeros((nb, 1, 3 * WIDTH), F32)
    zero_lora = jnp.zeros((nb, 1, 4 * LANES), F32)
    feats = _prep_call(p_main, zero_rkv, zero_lora, prep_w, nb, t, 256, True)
    rw, s_fin = _chunk_call(feats, lnw, lnb, ones_bd, tri, nb, t)
    kidx = p_idx[:, WIDTH:WIDTH + IDX_DIM]
    zpad = jnp.zeros_like(kidx)
    kz = _split(jnp.concatenate([kidx, zpad], axis=1)) + _split(jnp.concatenate([zpad, kidx], axis=1))
    k_p, v_p = p_main[:, C_AK:C_AK + WIDTH], p_main[:, C_AV:C_AV + WIDTH]
    att = _attn_call(p_idx, kz, p_main, k_p.astype(BF16), v_p.astype(BF16), nb, t)
    y_p = finish(xp, p_main, rw, att, 1024, 512, 512)
    blocks = jnp.stack([s_fin[:, :, :HEAD_DIM, :HEAD_DIM], s_fin[:, :, HEAD_DIM:, HEAD_DIM:]], axis=2)
    wkv_p = jnp.swapaxes(blocks.reshape(nb, HEADS, HEAD_DIM, HEAD_DIM), -1, -2)
    last = p_main.reshape(nb, t, N_MAIN)[:, -1]
    shift_p = _ungroup_cols(last[:, :3 * WIDTH], last[:, C_LORA:])

    xs = x_sample.reshape(ns, D_MODEL)
    ps_main, ps_idx = project(xs, ns)
    prev_rkv, prev_lora = _regroup_cols(state_shift)
    sf = _prep_call(ps_main, prev_rkv, prev_lora, prep_w, 1, ns, ns, False)
    logw_s, kap_s, bb_s, kt_s, r_s, v_s, bonus_s, g_s = sf
    wkv_s, y_s = _step_call(state_wkv, logw_s, kap_s, bb_s, kt_s, r_s, v_s)
    rw_s = _post_call(y_s.reshape(ns, WIDTH), bonus_s, g_s, lnw, lnb, ones_bd)

    npages = page_table.shape[1]
    past = npages * PAGE
    q_i = ps_idx[:, :WIDTH].reshape(ns, HEADS, IDX_DIM)
    w_i = ps_idx[:, WIDTH + IDX_DIM:WIDTH + IDX_DIM + HEADS].reshape(ns, HEADS, 1)
    kidx_s = ps_idx[:, WIDTH:WIDTH + IDX_DIM]
    knew = jnp.pad(kidx_s[:, None, :], ((0, 0), (0, PAGE - 1), (0, 0)))
    scores = _sidx_call(page_table, q_i, w_i, knew, cache_kidx)
    bias = _ssel_call(scores.reshape(ns, SCOLS), past + 1).reshape(ns, SROWS, PAGE)
    k_s, v_s_att = ps_main[:, C_AK:C_AK + WIDTH], ps_main[:, C_AV:C_AV + WIDTH]
    rows = PAGE * HEADS
    expand = jnp.asarray(np.arange(PAGE)[:, None] == (np.arange(rows)[None, :] // HEADS), BF16)
    att_s = _sattn_call(page_table, ps_main[:, C_AQ:C_AQ + WIDTH].reshape(ns, HEADS, HEAD_DIM),
                        k_s.reshape(ns, HEADS, HEAD_DIM), v_s_att.reshape(ns, HEADS, HEAD_DIM), bias, expand,
                        cache_k.reshape(-1, HEAD_DIM), cache_v.reshape(-1, HEAD_DIM))
    y_smp = finish(xs, ps_main, rw_s, att_s.reshape(ns, WIDTH).astype(BF16), ns, 512, ns)
    shift_s = _ungroup_cols(ps_main[:, :3 * WIDTH], ps_main[:, C_LORA:])

    hd = (HEADS, HEAD_DIM)
    return (y_p.reshape(nb, t, D_MODEL), y_smp.reshape(ns, 1, D_MODEL),
            k_p.reshape(nb, t, *hd), v_p.reshape(nb, t, *hd), kidx.reshape(nb, t, IDX_DIM),
            wkv_p, shift_p,
            k_s.reshape(ns, 1, *hd), v_s_att.reshape(ns, 1, *hd), kidx_s.reshape(ns, 1, IDX_DIM),
            wkv_s, shift_s)
```

```python
import functools

import numpy as np
import jax
import jax.numpy as jnp
from jax import lax
from jax.experimental import pallas as pl
from jax.experimental.pallas import tpu as pltpu

F32, BF16, I32 = jnp.float32, jnp.bfloat16, jnp.int32

D_MODEL = 2048
HEADS = 16
HEAD_DIM = 64
WIDTH = HEADS * HEAD_DIM
LORA_SMALL = 96
LORA_GATE = 256
RWKV_COLS = 3 * WIDTH + 2 * LORA_SMALL + LORA_GATE
IDX_DIM = 64
TOPK = 256
PAGE = 128
RMS_EPS = 1e-6
GN_EPS = 64e-5

LANES = 128
VMEM_BYTES = 64 * 1024 * 1024
PAIRS = WIDTH // LANES

C_R, C_K, C_V, C_AQ, C_AK, C_AV, C_GA, C_GB, C_LORA = 0, 1024, 2048, 3072, 4096, 5120, 6144, 8192, 10240
N_MAIN = 10752
N_IDX = 1152

INT_MIN = int(np.iinfo(np.int32).min)
INT_MAX = int(np.iinfo(np.int32).max)
NEG = -1e30

_NN = (((1,), (0,)), ((), ()))
_NT = (((1,), (1,)), ((), ()))


def _params(sem, vmem_mb):
    return pltpu.CompilerParams(dimension_semantics=sem, vmem_limit_bytes=vmem_mb * 1024 * 1024)


def _dot(a, b, dims=_NN):
    return lax.dot_general(a, b, dims, preferred_element_type=F32)


def _split(x):
    hi = x.astype(BF16)
    return hi, (x - hi.astype(F32)).astype(BF16)


def _mm3(ap, bp, dims=_NN):
    return _dot(ap[0], bp[0], dims) + (_dot(ap[0], bp[1], dims) + _dot(ap[1], bp[0], dims))


def _sigmoid(x):
    return 1.0 / (1.0 + jnp.exp(-x))


def _headsum(x, ones_bd):
    hi, lo = _split(x)
    return _dot(hi, ones_bd) + _dot(lo, ones_bd)


def _headsum_full(x, ones_bd):
    return jnp.concatenate(
        [_headsum(x[:, p * LANES:(p + 1) * LANES], ones_bd) for p in range(PAIRS)], axis=1)


def _rms_kernel(x_ref, g_ref, hi_ref, lo_ref):
    x = x_ref[...]
    y = x * lax.rsqrt(jnp.mean(x * x, axis=-1, keepdims=True) + RMS_EPS) * g_ref[...]
    hi = y.astype(BF16)
    hi_ref[...] = hi
    lo_ref[...] = (y - hi.astype(F32)).astype(BF16)


def _rms_call(x, g, tm):
    m, d = x.shape
    spec = pl.BlockSpec((tm, d), lambda i: (i, 0))
    return pl.pallas_call(
        _rms_kernel,
        out_shape=(jax.ShapeDtypeStruct((m, d), BF16), jax.ShapeDtypeStruct((m, d), BF16)),
        grid=(m // tm,),
        in_specs=[spec, pl.BlockSpec((1, d), lambda i: (0, 0))],
        out_specs=(spec, spec),
        compiler_params=_params(("arbitrary",), 32),
        name="rms_norm",
    )(x, g)


def _mm_kernel(*refs, na, nb, terms):
    a, b, o = refs[:na], refs[na:na + nb], refs[na + nb]
    acc = None
    for ia, ib in terms:
        d = _dot(a[ia][...], b[ib][...])
        acc = d if acc is None else acc + d
    o[...] = acc.astype(o.dtype)


def _mm_call(a_list, b_list, terms, out_dtype, tm, tn, name):
    m, k = a_list[0].shape
    n = b_list[0].shape[1]
    a_spec = pl.BlockSpec((tm, k), lambda j, i: (i, 0))
    b_spec = pl.BlockSpec((k, tn), lambda j, i: (0, j))
    return pl.pallas_call(
        functools.partial(_mm_kernel, na=len(a_list), nb=len(b_list), terms=terms),
        out_shape=jax.ShapeDtypeStruct((m, n), out_dtype),
        grid=(n // tn, m // tm),
        in_specs=[a_spec] * len(a_list) + [b_spec] * len(b_list),
        out_specs=pl.BlockSpec((tm, tn), lambda j, i: (i, j)),
        compiler_params=_params(("arbitrary", "arbitrary"), 48),
        name=name,
    )(*a_list, *b_list)


def _shift_rows(z, first_row):
    rows = lax.broadcasted_iota(I32, z.shape, 0)
    return jnp.where(rows == 0, first_row, pltpu.roll(z, 1, 0))


def _prep_kernel(rkv_ref, lora_ref, p_rkv_ref, p_lora_ref, mu_rkv_ref, mu_lora_ref, w0_ref, a0_ref,
                 kk_ref, ka_ref, rk_ref, ww2_ref, wa2_ref, wg2_ref, ones_ref,
                 logw_ref, kap_ref, bb_ref, kt_ref, r_ref, v_ref, bonus_ref, g_ref,
                 c_rkv, c_lora, *, carry):
    z = rkv_ref[...]
    zl = lora_ref[...]
    if carry:
        @pl.when(pl.program_id(1) == 0)
        def _():
            c_rkv[...] = p_rkv_ref[0]
            c_lora[...] = p_lora_ref[0]
        zp = _shift_rows(z, c_rkv[...])
        zlp = _shift_rows(zl, c_lora[...])
        tt = z.shape[0]
        c_rkv[...] = z[tt - 1:tt, :]
        c_lora[...] = zl[tt - 1:tt, :]
    else:
        zp = p_rkv_ref[...]
        zlp = p_lora_ref[...]
    zs = z + (zp - z) * mu_rkv_ref[...]
    zls = zl + (zlp - zl) * mu_lora_ref[...]
    r, k, v = zs[:, 0:WIDTH], zs[:, WIDTH:2 * WIDTH], zs[:, 2 * WIDTH:3 * WIDTH]
    wd, ad, gd = zls[:, 0:LANES], zls[:, LANES:2 * LANES], zls[:, 2 * LANES:4 * LANES]
    ones_bd = ones_ref[...]

    lw = w0_ref[...] + _dot(jnp.tanh(wd).astype(BF16), ww2_ref[...])
    nlw = -lw
    softplus = jnp.maximum(nlw, 0.0) + jnp.log(1.0 + jnp.exp(-jnp.abs(nlw)))
    logw_ref[...] = -jnp.exp(-softplus - 0.5)
    a = _sigmoid(a0_ref[...] + _dot(ad.astype(BF16), wa2_ref[...]))
    g_ref[...] = _dot(_sigmoid(gd).astype(BF16), wg2_ref[...])
    kkv = k * kk_ref[...]
    norm = jnp.sqrt(_headsum_full(kkv * kkv, ones_bd))
    kap = kkv / jnp.maximum(norm, 1e-12)
    kt = k * (1.0 + (a - 1.0) * ka_ref[...])
    kap_ref[...] = kap
    bb_ref[...] = kap * a
    kt_ref[...] = kt
    r_ref[...] = r
    v_ref[...] = v
    bonus_ref[...] = _headsum_full(r * kt * rk_ref[...], ones_bd) * v


def _prep_call(p_main, prev_rkv, prev_lora, wts, nb, t, tt, carry):
    m = nb * t
    nt = t // tt
    tok = lambda n, cb: pl.BlockSpec((tt, n), lambda b, i: (b * nt + i, cb))
    if carry:
        prev_specs = [pl.BlockSpec((1, 1, 3 * WIDTH), lambda b, i: (b, 0, 0)),
                      pl.BlockSpec((1, 1, 4 * LANES), lambda b, i: (b, 0, 0))]
    else:
        prev_specs = [tok(3 * WIDTH, 0), tok(4 * LANES, 0)]
    full = lambda a: pl.BlockSpec(a.shape, lambda b, i: (0,) * a.ndim)
    out_spec = tok(WIDTH, 0)
    return pl.pallas_call(
        functools.partial(_prep_kernel, carry=carry),
        out_shape=tuple(jax.ShapeDtypeStruct((m, WIDTH), F32) for _ in range(8)),
        grid=(nb, nt),
        in_specs=[tok(3 * WIDTH, 0), tok(4 * LANES, C_LORA // (4 * LANES))] + prev_specs + [full(a) for a in wts],
        out_specs=(out_spec,) * 8,
        scratch_shapes=[pltpu.VMEM((1, 3 * WIDTH), F32), pltpu.VMEM((1, 4 * LANES), F32)],
        compiler_params=_params(("arbitrary", "arbitrary"), 48),
        name="rwkv_prep",
    )(p_main, p_main, prev_rkv, prev_lora, *wts)


def _post_math(y, bonus, g, lnw, lnb, ones_bd):
    mu = _headsum(y, ones_bd) * (1.0 / HEAD_DIM)
    d = y - mu
    var = _headsum(d * d, ones_bd) * (1.0 / HEAD_DIM)
    yn = d * lax.rsqrt(var + GN_EPS) * lnw + lnb
    return ((yn + bonus) * g).astype(BF16)


def _post_kernel(y_ref, bonus_ref, g_ref, lnw_ref, lnb_ref, ones_ref, o_ref):
    for p in range(PAIRS):
        sl = slice(p * LANES, (p + 1) * LANES)
        o_ref[:, sl] = _post_math(y_ref[:, sl], bonus_ref[:, sl], g_ref[:, sl], lnw_ref[:, sl],
                                  lnb_ref[:, sl], ones_ref[...])


def _post_call(y, bonus, g, lnw, lnb, ones_bd):
    return pl.pallas_call(
        _post_kernel,
        out_shape=jax.ShapeDtypeStruct(y.shape, BF16),
        name="rwkv_post",
    )(y, bonus, g, lnw, lnb, ones_bd)


CHUNK = 64


def _chunk_kernel(logw_ref, kap_ref, bb_ref, kt_ref, r_ref, v_ref, bonus_ref, g_ref, lnw_ref, lnb_ref,
                  ones_ref, tri_ref, rw_ref, sfin_ref, s_ref):
    c = pl.program_id(1)

    @pl.when(c == 0)
    def _():
        s_ref[...] = jnp.zeros_like(s_ref)

    lane = lax.broadcasted_iota(I32, (1, LANES), 1)
    m0 = (lane < HEAD_DIM).astype(F32)
    m1 = 1.0 - m0
    rr = lax.broadcasted_iota(I32, (LANES, LANES), 0)
    cc = lax.broadcasted_iota(I32, (LANES, LANES), 1)
    strict = (cc & (CHUNK - 1)) < (rr & (CHUNK - 1))
    incl = (cc & (CHUNK - 1)) <= (rr & (CHUNK - 1))
    eye = (rr == cc).astype(F32)
    ones_bd = ones_ref[...]
    tri = tri_ref[...]

    def stack(x):
        return jnp.concatenate([x * m0, x * m1], axis=0)

    for p in range(PAIRS):
        sl = slice(p * LANES, (p + 1) * LANES)
        lw = logw_ref[:, sl]
        h1 = lw.astype(BF16)
        r1 = lw - h1.astype(F32)
        h2 = r1.astype(BF16)
        h3 = (r1 - h2.astype(F32)).astype(BF16)
        lam = _dot(tri, h1) + (_dot(tri, h2) + _dot(tri, h3))
        lam_c = lam[CHUNK - 1:CHUNK, :]
        e_in = jnp.exp(lam)
        e_ex = jnp.exp(lam - lw)
        e_inv = jnp.exp(-lam)
        e_rem = jnp.exp(lam_c - lam)
        kap, bb, kt = kap_ref[:, sl], bb_ref[:, sl], kt_ref[:, sl]
        s_kh = stack(kap * e_ex)
        s_rh = stack(r_ref[:, sl] * e_in)
        s_v = stack(v_ref[:, sl]).astype(BF16)

        lhs = jnp.concatenate([s_kh, s_rh], axis=0).astype(BF16)
        rhs = jnp.concatenate([stack(bb * e_inv), stack(kt * e_inv)], axis=0).astype(BF16)
        gram = _dot(lhs, rhs, _NT)
        n2 = 2 * CHUNK
        l_b = jnp.where(strict, gram[:n2, :n2], 0.0)
        l_k = jnp.where(strict, gram[:n2, n2:], 0.0).astype(BF16)
        a_b = jnp.where(incl, gram[n2:, :n2], 0.0).astype(BF16)
        a_k = jnp.where(incl, gram[n2:, n2:], 0.0).astype(BF16)

        pw = -l_b
        tinv = eye + pw
        for _ in range(5):
            pwb = pw.astype(BF16)
            pw = _dot(pwb, pwb)
            tinv = tinv + _dot(tinv.astype(BF16), pw.astype(BF16))
        tinv = tinv.astype(BF16)

        wu = _dot(tinv, jnp.concatenate([s_kh.astype(BF16), _dot(l_k, s_v).astype(BF16)], axis=1)).astype(BF16)
        bt_t = stack(bb * e_rem).T.astype(BF16)
        kt_t = stack(kt * e_rem).T.astype(BF16)
        mn = _dot(bt_t, wu)
        m_mat = eye * jnp.exp(lam_c) - mn[:, :LANES]
        n_mat = _dot(kt_t, s_v) - mn[:, LANES:]
        ab_wu = _dot(a_b, wu)
        s_ry = s_rh - ab_wu[:, :LANES]
        s_y0 = _dot(a_k, s_v) - ab_wu[:, LANES:]

        upd = _dot(jnp.concatenate([s_ry, m_mat], axis=0).astype(BF16), s_ref[p].astype(BF16))
        s_y = upd[:n2] + s_y0
        y = s_y[0:CHUNK, :] + s_y[CHUNK:2 * CHUNK, :]
        s_ref[p] = upd[n2:] + n_mat

        rw_ref[:, sl] = _post_math(y, bonus_ref[:, sl], g_ref[:, sl], lnw_ref[:, sl], lnb_ref[:, sl], ones_bd)

    @pl.when(c == pl.num_programs(1) - 1)
    def _():
        sfin_ref[0] = s_ref[...]


def _chunk_call(feats, lnw, lnb, ones_bd, tri, nb, t):
    nc = t // CHUNK
    tok = pl.BlockSpec((CHUNK, WIDTH), lambda b, c: (b * nc + c, 0))
    full = lambda a: pl.BlockSpec(a.shape, lambda b, c: (0,) * a.ndim)
    return pl.pallas_call(
        _chunk_kernel,
        out_shape=(jax.ShapeDtypeStruct((nb * t, WIDTH), BF16),
                   jax.ShapeDtypeStruct((nb, PAIRS, LANES, LANES), F32)),
        grid=(nb, nc),
        in_specs=[tok] * 8 + [full(lnw), full(lnb), full(ones_bd), full(tri)],
        out_specs=(tok, pl.BlockSpec((1, PAIRS, LANES, LANES), lambda b, c: (b, 0, 0, 0))),
        scratch_shapes=[pltpu.VMEM((PAIRS, LANES, LANES), F32)],
        compiler_params=_params(("arbitrary", "arbitrary"), 32),
        name="rwkv_chunk_scan",
    )(*feats, lnw, lnb, ones_bd, tri)


def _step_kernel(s_ref, logw_ref, kap_ref, bb_ref, kt_ref, r_ref, v_ref, so_ref, y_ref):
    s = s_ref[0]
    sa = -jnp.sum(s * kap_ref[0], axis=-1, keepdims=True)
    s_new = s * jnp.exp(logw_ref[0]) + sa * bb_ref[0] + v_ref[0] * kt_ref[0]
    so_ref[0] = s_new
    y_ref[0] = jnp.sum(s_new * r_ref[0], axis=-1, keepdims=True)


def _step_call(state, logw, kap, bb, kt, r, v):
    nb = state.shape[0]
    row = lambda x: x.reshape(nb, HEADS, 1, HEAD_DIM)
    s_spec = pl.BlockSpec((1, HEADS, HEAD_DIM, HEAD_DIM), lambda b: (b, 0, 0, 0))
    r_spec = pl.BlockSpec((1, HEADS, 1, HEAD_DIM), lambda b: (b, 0, 0, 0))
    c_spec = pl.BlockSpec((1, HEADS, HEAD_DIM, 1), lambda b: (b, 0, 0, 0))
    return pl.pallas_call(
        _step_kernel,
        out_shape=(jax.ShapeDtypeStruct(state.shape, F32), jax.ShapeDtypeStruct((nb, HEADS, HEAD_DIM, 1), F32)),
        grid=(nb,),
        in_specs=[s_spec] + [r_spec] * 5 + [c_spec],
        out_specs=(s_spec, c_spec),
        compiler_params=_params(("arbitrary",), 32),
        name="rwkv_step",
    )(state, row(logw), row(kap), row(bb), row(kt), row(r), v.reshape(nb, HEADS, HEAD_DIM, 1))


def _sortable(x):
    x = jnp.where(x == 0.0, 0.0, x)
    bits = pltpu.bitcast(x, I32)
    return jnp.where(bits < 0, bits ^ jnp.int32(0x7FFFFFFF), bits)


def _select_topk(keys_ref, nkc, rows, kc_size, col_of, idx_bits):
    def count(pred):
        def body(kc, acc):
            x = jnp.where(pred(kc), 1.0, 0.0)
            part = x[:, 0:LANES]
            for q in range(1, kc_size // LANES):
                part = part + x[:, q * LANES:(q + 1) * LANES]
            return acc + part
        acc = lax.fori_loop(0, nkc, body, jnp.zeros((rows, LANES), F32))
        return jnp.sum(acc, axis=-1, keepdims=True)

    kf = float(TOPK)
    thr = jnp.where(count(lambda kc: keys_ref[kc] >= 0) >= kf, jnp.int32(0), jnp.int32(INT_MIN))

    def bit_body(it, thr):
        cand = thr + lax.shift_left(jnp.int32(1), jnp.int32(30) - it)
        return jnp.where(count(lambda kc: keys_ref[kc] >= cand) >= kf, cand, thr)

    thr = lax.fori_loop(0, 31, bit_body, thr)
    need = kf - count(lambda kc: keys_ref[kc] > thr)
    split = (count(lambda kc: keys_ref[kc] == thr) > need) & (thr > jnp.int32(INT_MIN))
    any_split = jnp.max(jnp.where(split, 1.0, 0.0)) > 0.0

    def tie_break():
        def tie_body(it, x):
            cand = x + lax.shift_left(jnp.int32(1), jnp.int32(idx_bits - 1) - it)
            c = count(lambda kc: (keys_ref[kc] == thr) & (col_of(kc) <= cand))
            return jnp.where(c < need, cand, x)

        return lax.fori_loop(0, idx_bits, tie_body, jnp.full((rows, 1), -1, I32)) + 1

    jb = lax.cond(any_split, tie_break, lambda: jnp.full((rows, 1), INT_MAX, I32))
    return thr, jb


TQ = 128
KC = 512


M_INIT = -1e30


def _idx_key_pack(kidx):
    hi, lo = _split(kidx)
    return jnp.concatenate([hi, lo, hi, jnp.zeros_like(hi)], axis=1)


def _attn_kernel(qidx_ref, widx_ref, kcat_ref, aq_ref, kt_ref, vb_ref,
                 att_ref, keys_ref, bias_ref, qcat_ref, d_ref, qm_ref, m_ref, l_ref, acc_ref):
    i = pl.program_id(1)
    nkc = lax.shift_right_logical((i + 1) * TQ + (KC - 1), int(np.log2(KC)))
    row = i * TQ + lax.broadcasted_iota(I32, (TQ, KC), 0)
    lane_kc = lax.broadcasted_iota(I32, (TQ, KC), 1)
    col_of = lambda kc: kc * KC + lane_kc
    head_rows = lambda h: slice(h * TQ, (h + 1) * TQ)
    first = lax.broadcasted_iota(I32, (1, LANES), 1) < HEAD_DIM

    for p in range(PAIRS):
        x = qidx_ref[:, p * LANES:(p + 1) * LANES]
        hi = x.astype(BF16).astype(F32)
        lo = x - hi
        hi_sw = pltpu.roll(hi, HEAD_DIM, 1)
        lo_sw = pltpu.roll(lo, HEAD_DIM, 1)
        for hh in range(2):
            rows = head_rows(2 * p + hh)
            own_hi, other_hi = (hi, hi_sw) if hh == 0 else (hi_sw, hi)
            qcat_ref[rows, 0:LANES] = jnp.where(first, own_hi, other_hi).astype(BF16)
            qcat_ref[rows, LANES:2 * LANES] = jnp.where(first, lo if hh == 0 else lo_sw, 0.0).astype(BF16)
    wv = widx_ref[:, IDX_DIM:IDX_DIM + HEADS] * (HEADS ** -0.5 * IDX_DIM ** -0.5)

    def score_body(kc, carry):
        d_ref[...] = _dot(qcat_ref[...], kcat_ref[kc])
        acc = jnp.zeros((TQ, KC), F32)
        for h in range(HEADS):
            acc = acc + jnp.maximum(d_ref[head_rows(h), :], 0.0) * wv[:, h:h + 1]
        keys_ref[kc] = jnp.where(col_of(kc) <= row, _sortable(acc), jnp.int32(INT_MIN))
        return carry

    lax.fori_loop(0, nkc, score_body, 0)

    thr, jb = _select_topk(keys_ref, nkc, TQ, KC, col_of, 12)

    def bias_body(kc, carry):
        key = keys_ref[kc]
        col = col_of(kc)
        sel = ((key > thr) | ((key == thr) & (col <= jb))) & (col <= row)
        bias_ref[kc] = jnp.where(sel, 0.0, 2.0 * M_INIT)
        return carry

    lax.fori_loop(0, nkc, bias_body, 0)

    for p in range(PAIRS):
        aq = aq_ref[:, p * LANES:(p + 1) * LANES] * (HEAD_DIM ** -0.5)
        qm_ref[p, 0:TQ, :] = jnp.where(first, aq, 0.0).astype(BF16)
        qm_ref[p, TQ:2 * TQ, :] = jnp.where(first, 0.0, aq).astype(BF16)
    def lane_tiles(x, op):
        out = x[:, 0:LANES]
        for q in range(1, KC // LANES):
            out = op(out, x[:, q * LANES:(q + 1) * LANES])
        return out

    halves = (slice(0, TQ), slice(TQ, 2 * TQ))

    l_ref[...] = jnp.full(l_ref.shape, M_INIT, F32)

    def max_body(kc, carry):
        bias = bias_ref[kc]
        for p in range(PAIRS):
            s = _dot(qm_ref[p], kt_ref[kc, p * LANES:(p + 1) * LANES, :])
            for rs in halves:
                l_ref[p, rs, :] = jnp.maximum(l_ref[p, rs, :], lane_tiles(s[rs] + bias, jnp.maximum))
        return carry

    lax.fori_loop(0, nkc, max_body, 0)
    for p in range(PAIRS):
        m_ref[p] = jnp.max(l_ref[p], axis=-1, keepdims=True)

    l_ref[...] = jnp.zeros(l_ref.shape, F32)
    acc_ref[...] = jnp.zeros(acc_ref.shape, F32)

    def pv_body(kc, carry):
        off = pl.multiple_of(kc * KC, KC)
        bias = bias_ref[kc]
        for p in range(PAIRS):
            sl = slice(p * LANES, (p + 1) * LANES)
            s = _dot(qm_ref[p], kt_ref[kc, sl, :])
            es = []
            for rs in halves:
                e = jnp.exp(s[rs] + bias - m_ref[p, rs, :])
                l_ref[p, rs, :] += lane_tiles(e, jnp.add)
                es.append(e.astype(BF16))
            acc_ref[p] += _dot(jnp.concatenate(es, axis=0), vb_ref[pl.ds(off, KC), sl])
        return carry

    lax.fori_loop(0, nkc, pv_body, 0)

    for p in range(PAIRS):
        o = acc_ref[p] / jnp.sum(l_ref[p], axis=-1, keepdims=True)
        att_ref[:, p * LANES:(p + 1) * LANES] = jnp.where(first, o[0:TQ], o[TQ:2 * TQ]).astype(BF16)


def _attn_call(p_idx, kcat_t, p_main, kt, vb, nb, t):
    nq = t // TQ
    nkc = t // KC
    qtok = lambda n, cb: pl.BlockSpec((TQ, n), lambda b, i: (b * nq + i, cb))
    seq3 = lambda n: pl.BlockSpec((nkc, n, KC), lambda b, i: (b, 0, 0), pipeline_mode=pl.Buffered(1))
    seq2 = pl.BlockSpec((t, WIDTH), lambda b, i: (b, 0), pipeline_mode=pl.Buffered(1))
    return pl.pallas_call(
        _attn_kernel,
        out_shape=jax.ShapeDtypeStruct((nb * t, WIDTH), BF16),
        grid=(nb, nq),
        in_specs=[qtok(WIDTH, 0), qtok(LANES, WIDTH // LANES), seq3(4 * IDX_DIM),
                  qtok(WIDTH, C_AQ // WIDTH), seq3(WIDTH), seq2],
        out_specs=qtok(WIDTH, 0),
        scratch_shapes=[pltpu.VMEM((nkc, TQ, KC), I32), pltpu.VMEM((nkc, TQ, KC), F32),
                        pltpu.VMEM((HEADS * TQ, 2 * LANES), BF16), pltpu.VMEM((HEADS * TQ, KC), F32),
                        pltpu.VMEM((PAIRS, 2 * TQ, LANES), BF16), pltpu.VMEM((PAIRS, 2 * TQ, 1), F32),
                        pltpu.VMEM((PAIRS, 2 * TQ, LANES), F32), pltpu.VMEM((PAIRS, 2 * TQ, LANES), F32)],
        compiler_params=_params(("arbitrary", "arbitrary"), 48),
        name="dsa_prompt_attention",
    )(p_idx, p_idx, kcat_t, p_main, kt, vb)


PG = 8
SROWS = 72
SCOLS = SROWS * PAGE


def _sidx_kernel(pt_ref, q_ref, w_ref, knew_ref, *rest, npages):
    pages, o_ref = rest[:PG], rest[PG]
    j = pl.program_id(1)
    qh, ql = _split(q_ref[0])
    w = w_ref[0] * (HEADS ** -0.5 * IDX_DIM ** -0.5)

    def score(kpage):
        kh, kl = _split(kpage)
        d = _dot(qh, kh) + (_dot(qh, kl) + _dot(ql, kh))
        return jnp.sum(jnp.maximum(d, 0.0) * w, axis=0, keepdims=True)

    @pl.when(j == 0)
    def _():
        o_ref[0] = jnp.zeros((SROWS, PAGE), F32)
        o_ref[0, npages:npages + 1, :] = score(knew_ref[0])

    for g in range(PG):
        o_ref[0, pl.ds(j * PG + g, 1), :] = score(pages[g][0])


def _sidx_call(page_table, q, w, knew, cache_kidx):
    nb, npages = page_table.shape
    one = lambda s: pl.BlockSpec((1,) + s, lambda b, j, pt: (b, 0, 0))
    page = lambda g: pl.BlockSpec((1, IDX_DIM, PAGE), lambda b, j, pt: (pt[b, j * PG + g], 0, 0))
    return pl.pallas_call(
        functools.partial(_sidx_kernel, npages=npages),
        out_shape=jax.ShapeDtypeStruct((nb, SROWS, PAGE), F32),
        grid_spec=pltpu.PrefetchScalarGridSpec(
            num_scalar_prefetch=1,
            grid=(nb, npages // PG),
            in_specs=[one((HEADS, IDX_DIM)), one((HEADS, 1)), one((IDX_DIM, PAGE))] + [page(g) for g in range(PG)],
            out_specs=one((SROWS, PAGE)),
        ),
        compiler_params=_params(("arbitrary", "arbitrary"), 32),
        name="dsa_sample_index_scores",
    )(page_table, q, w, knew, *([cache_kidx] * PG))


SKC = 512
SNKC = SCOLS // SKC


def _ssel_kernel(sc_ref, bias_ref, keys_ref, *, n_valid):
    rows = sc_ref.shape[0]
    lane_kc = lax.broadcasted_iota(I32, (rows, SKC), 1)
    col_of = lambda kc: kc * SKC + lane_kc
    for c in range(SNKC):
        x = sc_ref[:, c * SKC:(c + 1) * SKC]
        keys_ref[c] = jnp.where(col_of(c) < n_valid, _sortable(x), jnp.int32(INT_MIN))
    thr, jb = _select_topk(keys_ref, SNKC, rows, SKC, col_of, 14)
    for c in range(SNKC):
        key = keys_ref[c]
        col = col_of(c)
        sel = ((key > thr) | ((key == thr) & (col <= jb))) & (col < n_valid)
        bias_ref[:, c * SKC:(c + 1) * SKC] = jnp.where(sel, 0.0, NEG)


def _ssel_call(scores, n_valid):
    rows = scores.shape[0]
    return pl.pallas_call(
        functools.partial(_ssel_kernel, n_valid=n_valid),
        out_shape=jax.ShapeDtypeStruct(scores.shape, F32),
        scratch_shapes=[pltpu.VMEM((SNKC, rows, SKC), I32)],
        name="dsa_sample_select",
    )(scores)


def _sattn_kernel(pt_ref, q_ref, kn_ref, vn_ref, bias_ref, *rest, npages):
    kpages, vpages = rest[:PG], rest[PG:2 * PG]
    o_ref, qb_ref, m_ref, l_ref, acc_ref = rest[2 * PG:]
    j = pl.program_id(1)
    scale = HEAD_DIM ** -0.5

    @pl.when(j == 0)
    def _():
        on = bias_ref[0, npages:npages + 1, 0:1] == 0.0
        first = on & (lax.broadcasted_iota(I32, (1, PAGE), 1) == 0)
        for h in range(HEADS):
            qh = q_ref[0, h] * scale
            qb_ref[h] = jnp.broadcast_to(qh, (HEAD_DIM, PAGE))
            s_new = jnp.sum(qh * kn_ref[0, h], axis=0, keepdims=True)
            m_ref[h] = jnp.broadcast_to(jnp.where(on, s_new, NEG), (1, PAGE))
            l_ref[h] = jnp.broadcast_to(jnp.where(on, 1.0, 0.0), (1, PAGE))
            acc_ref[h] = jnp.where(first, vn_ref[0, h], 0.0)

    for g in range(PG):
        on = bias_ref[0, pl.ds(j * PG + g, 1), :] == 0.0
        for h in range(HEADS):
            s = jnp.sum(qb_ref[h] * kpages[g][0, h], axis=0, keepdims=True)
            m = m_ref[h]
            m_new = jnp.maximum(m, jnp.max(jnp.where(on, s, NEG), axis=-1, keepdims=True))
            e = jnp.where(on, jnp.exp(s - m_new), 0.0)
            alpha = jnp.exp(m - m_new)
            m_ref[h] = m_new
            l_ref[h] = alpha * l_ref[h] + jnp.sum(e, axis=-1, keepdims=True)
            acc_ref[h] = alpha * acc_ref[h] + e * vpages[g][0, h]

    @pl.when(j == pl.num_programs(1) - 1)
    def _():
        for h in range(HEADS):
            o_ref[0, h] = jnp.sum(acc_ref[h], axis=-1, keepdims=True) / l_ref[h][:, 0:1]


def _sattn_call(page_table, q, kn, vn, bias, ck, cv):
    nb, npages = page_table.shape
    one = lambda s: pl.BlockSpec((1,) + s, lambda b, j, pt: (b,) + (0,) * len(s))
    col = one((HEADS, HEAD_DIM, 1))
    page = lambda g: pl.BlockSpec((1, HEADS, HEAD_DIM, PAGE), lambda b, j, pt: (pt[b, j * PG + g], 0, 0, 0))
    return pl.pallas_call(
        functools.partial(_sattn_kernel, npages=npages),
        out_shape=jax.ShapeDtypeStruct((nb, HEADS, HEAD_DIM, 1), F32),
        grid_spec=pltpu.PrefetchScalarGridSpec(
            num_scalar_prefetch=1,
            grid=(nb, npages // PG),
            in_specs=[col, col, col, one((SROWS, PAGE))] + [page(g) for g in range(PG)] * 2,
            out_specs=col,
            scratch_shapes=[pltpu.VMEM((HEADS, HEAD_DIM, PAGE), F32), pltpu.VMEM((HEADS, 1, PAGE), F32),
                            pltpu.VMEM((HEADS, 1, PAGE), F32), pltpu.VMEM((HEADS, HEAD_DIM, PAGE), F32)],
        ),
        compiler_params=_params(("arbitrary", "arbitrary"), 40),
        name="dsa_sample_attention",
    )(page_table, q, kn, vn, bias, *([ck] * PG), *([cv] * PG))


def _merge_kernel(rw_ref, att_ref, wr_ref, wa_ref, ga_ref, gb_ref, o_ref):
    yr = _dot(rw_ref[...], wr_ref[...])
    ya = _dot(att_ref[...], wa_ref[...])
    o_ref[...] = (_sigmoid(ga_ref[...]) * yr + _sigmoid(gb_ref[...]) * ya).astype(BF16)


def _merge_call(rw, att, w_r, w_a, p_main, tm, tn):
    m = rw.shape[0]
    lhs = pl.BlockSpec((tm, WIDTH), lambda j, i: (i, 0))
    rhs = pl.BlockSpec((WIDTH, tn), lambda j, i: (0, j))
    gate = lambda c0: pl.BlockSpec((tm, tn), lambda j, i: (i, c0 // tn + j))
    return pl.pallas_call(
        _merge_kernel,
        out_shape=jax.ShapeDtypeStruct((m, D_MODEL), BF16),
        grid=(D_MODEL // tn, m // tm),
        in_specs=[lhs, lhs, rhs, rhs, gate(C_GA), gate(C_GB)],
        out_specs=pl.BlockSpec((tm, tn), lambda j, i: (i, j)),
        compiler_params=_params(("arbitrary", "arbitrary"), 32),
        name="mixer_merge",
    )(rw, att, w_r, w_a, p_main, p_main)


def _rms(x, g):
    return x * lax.rsqrt(jnp.mean(x * x, axis=-1, keepdims=True) + RMS_EPS) * g


def _outproj_kernel(m_ref, w_ref, x_ref, gpost_ref, gpre_ref, h_ref, f_ref):
    h = x_ref[...] + _rms(_dot(m_ref[...], w_ref[...]), gpost_ref[...])
    h_ref[...] = h
    f_ref[...] = _rms(h, gpre_ref[...]).astype(BF16)


def _outproj_call(mix, w_out, x, g_post, g_pre, tm):
    m = x.shape[0]
    tok = pl.BlockSpec((tm, D_MODEL), lambda i: (i, 0))
    full = lambda a: pl.BlockSpec(a.shape, lambda i: (0, 0))
    return pl.pallas_call(
        _outproj_kernel,
        out_shape=(jax.ShapeDtypeStruct((m, D_MODEL), F32), jax.ShapeDtypeStruct((m, D_MODEL), BF16)),
        grid=(m // tm,),
        in_specs=[tok, full(w_out), tok, full(g_post), full(g_pre)],
        out_specs=(tok, tok),
        compiler_params=_params(("arbitrary",), 40),
        name="mixer_out_proj",
    )(mix, w_out, x, g_post, g_pre)


def _ffn_kernel(f_ref, wg_ref, wu_ref, wd_ref, h_ref, gp_ref, y_ref, acc_ref):
    j = pl.program_id(1)
    f = f_ref[...]
    gate = _dot(f, wg_ref[...])
    t = (gate * _sigmoid(gate) * _dot(f, wu_ref[...])).astype(BF16)
    part = _dot(t, wd_ref[...])

    @pl.when(j == 0)
    def _():
        acc_ref[...] = part

    @pl.when(j > 0)
    def _():
        acc_ref[...] += part

    @pl.when(j == pl.num_programs(1) - 1)
    def _():
        y_ref[...] = h_ref[...] + _rms(acc_ref[...], gp_ref[...])


def _ffn_call(f, w_gate, w_up, w_down, h, g_post, tm, tf):
    m = f.shape[0]
    d_ff = w_gate.shape[1]
    tok = pl.BlockSpec((tm, D_MODEL), lambda i, j: (i, 0))
    return pl.pallas_call(
        _ffn_kernel,
        out_shape=jax.ShapeDtypeStruct((m, D_MODEL), F32),
        grid=(m // tm, d_ff // tf),
        in_specs=[tok, pl.BlockSpec((D_MODEL, tf), lambda i, j: (0, j)), pl.BlockSpec((D_MODEL, tf), lambda i, j: (0, j)),
                  pl.BlockSpec((tf, D_MODEL), lambda i, j: (j, 0)), tok, pl.BlockSpec((1, D_MODEL), lambda i, j: (0, 0))],
        out_specs=tok,
        scratch_shapes=[pltpu.VMEM((tm, D_MODEL), F32)],
        compiler_params=_params(("arbitrary", "arbitrary"), 48),
        name="swiglu_ffn",
    )(f, w_gate, w_up, w_down, h, g_post)


def _pad_cols(a, n):
    return jnp.pad(a, ((0, 0), (0, n - a.shape[1])))


def _regroup_cols(a):
    w3, ls = 3 * WIDTH, LORA_SMALL
    lora = jnp.concatenate([_pad_cols(a[:, w3:w3 + ls], LANES), _pad_cols(a[:, w3 + ls:w3 + 2 * ls], LANES),
                            a[:, w3 + 2 * ls:]], axis=1)
    return a[:, :w3], lora


def _ungroup_cols(rkv, lora):
    ls = LORA_SMALL
    return jnp.concatenate([rkv, lora[:, :ls], lora[:, LANES:LANES + ls], lora[:, 2 * LANES:]], axis=1)


def kernel(x_prompt, x_sample, cache_k, cache_v, cache_kidx, state_wkv, state_shift, page_table, g_pre_mix, w_in,
           rwkv_mu, w0, w_w2, a0, w_a2, w_g2, k_k, k_a, r_k, ln_x_w, ln_x_b, w_o_rwkv, w_o_attn, w_out,
           g_post_mix, g_pre_ffn, w_gate, w_up, w_down, g_post_ffn):
    nb, t, _ = x_prompt.shape
    ns = x_sample.shape[0]
    row = lambda a: a.reshape(1, -1).astype(F32)

    o_q = RWKV_COLS
    o_qi = o_q + 3 * WIDTH
    o_ki = o_qi + WIDTH
    o_ga = o_ki + IDX_DIM + HEADS
    w_rkv, w_lora = _regroup_cols(w_in[:, :RWKV_COLS])
    w_main = jnp.concatenate([w_rkv, w_in[:, o_q:o_qi], w_in[:, o_ga:], w_lora], axis=1).astype(BF16)
    w_idx_hi, w_idx_lo = _split(jnp.concatenate([w_in[:, o_qi:o_ki], _pad_cols(w_in[:, o_ki:o_ga], LANES)], axis=1))
    mu_rkv, mu_lora = _regroup_cols(row(rwkv_mu))
    pad_rows = lambda a: jnp.pad(a, ((0, LANES - a.shape[0]), (0, 0))).astype(BF16)
    lane = np.arange(LANES)
    ones_bd = jnp.asarray((lane[:, None] // HEAD_DIM) == (lane[None, :] // HEAD_DIM), BF16)
    tri = jnp.asarray(np.arange(CHUNK)[:, None] >= np.arange(CHUNK)[None, :], BF16)
    prep_w = [mu_rkv, mu_lora, row(w0), row(a0), row(k_k), row(k_a), row(r_k), pad_rows(w_w2), pad_rows(w_a2),
              w_g2.astype(BF16), ones_bd]
    lnw, lnb = row(ln_x_w), row(ln_x_b)
    w_or, w_oa, w_o = w_o_rwkv.astype(BF16), w_o_attn.astype(BF16), w_out.astype(BF16)
    w_g, w_u, w_d = w_gate.astype(BF16), w_up.astype(BF16), w_down.astype(BF16)
    gains = [row(g_pre_mix), row(g_post_mix), row(g_pre_ffn), row(g_post_ffn)]

    def project(x2d, tm):
        u_hi, u_lo = _rms_call(x2d, gains[0], min(tm, 512))
        p_main = _mm_call([u_hi], [w_main], [(0, 0)], F32, tm, 512, "in_proj")
        p_idx = _mm_call([u_hi, u_lo], [w_idx_hi, w_idx_lo], [(0, 0), (0, 1), (1, 0)], F32, tm, 384, "in_proj_idx")
        return p_main, p_idx

    def finish(x2d, p_main, rw, att, tm, tn, tm_ffn):
        mix = _merge_call(rw, att, w_or, w_oa, p_main, tm, tn)
        h, f = _outproj_call(mix, w_o, x2d, gains[1], gains[2], min(tm, 256))
        return _ffn_call(f, w_g, w_u, w_d, h, gains[3], tm_ffn, 512)

    m = nb * t
    xp = x_prompt.reshape(m, D_MODEL)
    p_main, p_idx = project(xp, 1024)
    zero_rkv = jnp.zeros((nb, 1, 3 * WIDTH), F32)
    zero_lora = jnp.zeros((nb, 1, 4 * LANES), F32)
    feats = _prep_call(p_main, zero_rkv, zero_lora, prep_w, nb, t, 256, True)
    rw, s_fin = _chunk_call(feats, lnw, lnb, ones_bd, tri, nb, t)
    kidx = p_idx[:, WIDTH:WIDTH + IDX_DIM]
    k_p, v_p = p_main[:, C_AK:C_AK + WIDTH], p_main[:, C_AV:C_AV + WIDTH]
    chunk_t = lambda a: jnp.swapaxes(a.reshape(m // KC, KC, a.shape[1]), 1, 2)
    att = _attn_call(p_idx, chunk_t(_idx_key_pack(kidx)), p_main, chunk_t(k_p.astype(BF16)), v_p.astype(BF16), nb, t)
    y_p = finish(xp, p_main, rw, att, 1024, 512, 512)
    blocks = jnp.stack([s_fin[:, :, :HEAD_DIM, :HEAD_DIM], s_fin[:, :, HEAD_DIM:, HEAD_DIM:]], axis=2)
    wkv_p = jnp.swapaxes(blocks.reshape(nb, HEADS, HEAD_DIM, HEAD_DIM), -1, -2)
    last = p_main.reshape(nb, t, N_MAIN)[:, -1]
    shift_p = _ungroup_cols(last[:, :3 * WIDTH], last[:, C_LORA:])

    xs = x_sample.reshape(ns, D_MODEL)
    ps_main, ps_idx = project(xs, ns)
    prev_rkv, prev_lora = _regroup_cols(state_shift)
    sf = _prep_call(ps_main, prev_rkv, prev_lora, prep_w, 1, ns, ns, False)
    logw_s, kap_s, bb_s, kt_s, r_s, v_s, bonus_s, g_s = sf
    wkv_s, y_s = _step_call(state_wkv, logw_s, kap_s, bb_s, kt_s, r_s, v_s)
    rw_s = _post_call(y_s.reshape(ns, WIDTH), bonus_s, g_s, lnw, lnb, ones_bd)

    npages = page_table.shape[1]
    past = npages * PAGE
    q_i = ps_idx[:, :WIDTH].reshape(ns, HEADS, IDX_DIM)
    w_i = ps_idx[:, WIDTH + IDX_DIM:WIDTH + IDX_DIM + HEADS].reshape(ns, HEADS, 1)
    kidx_s = ps_idx[:, WIDTH:WIDTH + IDX_DIM]
    knew = jnp.pad(kidx_s[:, :, None], ((0, 0), (0, 0), (0, PAGE - 1)))
    scores = _sidx_call(page_table, q_i, w_i, knew, jnp.transpose(cache_kidx, (0, 2, 1)))
    bias = _ssel_call(scores.reshape(ns, SCOLS), past + 1).reshape(ns, SROWS, PAGE)
    k_s, v_s_att = ps_main[:, C_AK:C_AK + WIDTH], ps_main[:, C_AV:C_AV + WIDTH]
    col = lambda a: a.reshape(ns, HEADS, HEAD_DIM, 1)
    att_s = _sattn_call(page_table, col(ps_main[:, C_AQ:C_AQ + WIDTH]), col(k_s), col(v_s_att), bias,
                        jnp.transpose(cache_k, (0, 2, 3, 1)), jnp.transpose(cache_v, (0, 2, 3, 1)))
    y_smp = finish(xs, ps_main, rw_s, att_s.reshape(ns, WIDTH).astype(BF16), ns, 512, ns)
    shift_s = _ungroup_cols(ps_main[:, :3 * WIDTH], ps_main[:, C_LORA:])

    hd = (HEADS, HEAD_DIM)
    return (y_p.reshape(nb, t, D_MODEL), y_smp.reshape(ns, 1, D_MODEL),
            k_p.reshape(nb, t, *hd), v_p.reshape(nb, t, *hd), kidx.reshape(nb, t, IDX_DIM),
            wkv_p, shift_p,
            k_s.reshape(ns, 1, *hd), v_s_att.reshape(ns, 1, *hd), kidx_s.reshape(ns, 1, IDX_DIM),
            wkv_s, shift_s)
```

```python
import functools

import numpy as np
import jax
import jax.numpy as jnp
from jax import lax
from jax.experimental import pallas as pl
from jax.experimental.pallas import tpu as pltpu

F32, BF16, I32 = jnp.float32, jnp.bfloat16, jnp.int32

D_MODEL = 2048
HEADS = 16
HEAD_DIM = 64
WIDTH = HEADS * HEAD_DIM
LORA_SMALL = 96
LORA_GATE = 256
RWKV_COLS = 3 * WIDTH + 2 * LORA_SMALL + LORA_GATE
IDX_DIM = 64
TOPK = 256
PAGE = 128
RMS_EPS = 1e-6
GN_EPS = 64e-5

LANES = 128
VMEM_BYTES = 64 * 1024 * 1024
PAIRS = WIDTH // LANES

C_R, C_K, C_V, C_AQ, C_GA, C_GB, C_LORA = 0, 1024, 2048, 3072, 4096, 6144, 8192
N_MAIN = 8704
N_IDX = 1152

INT_MIN = int(np.iinfo(np.int32).min)
INT_MAX = int(np.iinfo(np.int32).max)
NEG = -1e30

_NN = (((1,), (0,)), ((), ()))
_NT = (((1,), (1,)), ((), ()))


def _params(sem, vmem_mb):
    return pltpu.CompilerParams(dimension_semantics=sem, vmem_limit_bytes=vmem_mb * 1024 * 1024)


def _dot(a, b, dims=_NN):
    return lax.dot_general(a, b, dims, preferred_element_type=F32)


def _split(x):
    hi = x.astype(BF16)
    return hi, (x - hi.astype(F32)).astype(BF16)


def _mm3(ap, bp, dims=_NN):
    return _dot(ap[0], bp[0], dims) + (_dot(ap[0], bp[1], dims) + _dot(ap[1], bp[0], dims))


def _sigmoid(x):
    return 1.0 / (1.0 + jnp.exp(-x))


def _headsum(x, ones_bd):
    hi, lo = _split(x)
    return _dot(hi, ones_bd) + _dot(lo, ones_bd)


def _headsum_full(x, ones_bd):
    return jnp.concatenate(
        [_headsum(x[:, p * LANES:(p + 1) * LANES], ones_bd) for p in range(PAIRS)], axis=1)


def _rms_kernel(x_ref, g_ref, hi_ref, lo_ref, *t_ref):
    x = x_ref[...]
    y = x * lax.rsqrt(jnp.mean(x * x, axis=-1, keepdims=True) + RMS_EPS) * g_ref[...]
    hi = y.astype(BF16)
    hi_ref[...] = hi
    lo_ref[...] = (y - hi.astype(F32)).astype(BF16)
    if t_ref:
        t_ref[0][...] = hi.astype(F32).T.astype(BF16)


def _rms_call(x, g, tm, transposed=False):
    m, d = x.shape
    spec = pl.BlockSpec((tm, d), lambda i: (i, 0))
    shapes = [jax.ShapeDtypeStruct((m, d), BF16), jax.ShapeDtypeStruct((m, d), BF16)]
    specs = [spec, spec]
    if transposed:
        shapes.append(jax.ShapeDtypeStruct((d, m), BF16))
        specs.append(pl.BlockSpec((d, tm), lambda i: (0, i)))
    return pl.pallas_call(
        _rms_kernel,
        out_shape=tuple(shapes),
        grid=(m // tm,),
        in_specs=[spec, pl.BlockSpec((1, d), lambda i: (0, 0))],
        out_specs=tuple(specs),
        compiler_params=_params(("arbitrary",), 40),
        name="rms_norm",
    )(x, g)


def _proj_t_kernel(w_ref, ut_ref, o32_ref, o16_ref):
    y = _dot(w_ref[...], ut_ref[...])
    o32_ref[0] = y
    o16_ref[0] = y.astype(BF16)


def _proj_t_call(w_t, u_t, nb, t, tn):
    n, d = w_t.shape
    nc = t // tn
    return pl.pallas_call(
        _proj_t_kernel,
        out_shape=(jax.ShapeDtypeStruct((nb, n, t), F32), jax.ShapeDtypeStruct((nb * nc, n, tn), BF16)),
        grid=(nb, nc),
        in_specs=[pl.BlockSpec((n, d), lambda b, j: (0, 0)), pl.BlockSpec((d, tn), lambda b, j: (0, b * nc + j))],
        out_specs=(pl.BlockSpec((1, n, tn), lambda b, j: (b, 0, j)),
                   pl.BlockSpec((1, n, tn), lambda b, j: (b * nc + j, 0, 0))),
        compiler_params=_params(("arbitrary", "arbitrary"), 40),
        name="in_proj_t",
    )(w_t, u_t)


def _mm_kernel(*refs, na, nb, terms):
    a, b, o = refs[:na], refs[na:na + nb], refs[na + nb]
    acc = None
    for ia, ib in terms:
        d = _dot(a[ia][...], b[ib][...])
        acc = d if acc is None else acc + d
    o[...] = acc.astype(o.dtype)


def _mm_call(a_list, b_list, terms, out_dtype, tm, tn, name):
    m, k = a_list[0].shape
    n = b_list[0].shape[1]
    a_spec = pl.BlockSpec((tm, k), lambda j, i: (i, 0))
    b_spec = pl.BlockSpec((k, tn), lambda j, i: (0, j))
    return pl.pallas_call(
        functools.partial(_mm_kernel, na=len(a_list), nb=len(b_list), terms=terms),
        out_shape=jax.ShapeDtypeStruct((m, n), out_dtype),
        grid=(n // tn, m // tm),
        in_specs=[a_spec] * len(a_list) + [b_spec] * len(b_list),
        out_specs=pl.BlockSpec((tm, tn), lambda j, i: (i, j)),
        compiler_params=_params(("arbitrary", "arbitrary"), 48),
        name=name,
    )(*a_list, *b_list)


def _shift_rows(z, first_row):
    rows = lax.broadcasted_iota(I32, z.shape, 0)
    return jnp.where(rows == 0, first_row, pltpu.roll(z, 1, 0))


def _prep_kernel(rkv_ref, lora_ref, p_rkv_ref, p_lora_ref, mu_rkv_ref, mu_lora_ref, w0_ref, a0_ref,
                 kk_ref, ka_ref, rk_ref, ww2_ref, wa2_ref, wg2_ref, ones_ref,
                 logw_ref, kap_ref, bb_ref, kt_ref, r_ref, v_ref, bonus_ref, g_ref,
                 c_rkv, c_lora, *, carry):
    z = rkv_ref[...]
    zl = lora_ref[...]
    if carry:
        @pl.when(pl.program_id(1) == 0)
        def _():
            c_rkv[...] = p_rkv_ref[0]
            c_lora[...] = p_lora_ref[0]
        zp = _shift_rows(z, c_rkv[...])
        zlp = _shift_rows(zl, c_lora[...])
        tt = z.shape[0]
        c_rkv[...] = z[tt - 1:tt, :]
        c_lora[...] = zl[tt - 1:tt, :]
    else:
        zp = p_rkv_ref[...]
        zlp = p_lora_ref[...]
    zs = z + (zp - z) * mu_rkv_ref[...]
    zls = zl + (zlp - zl) * mu_lora_ref[...]
    r, k, v = zs[:, 0:WIDTH], zs[:, WIDTH:2 * WIDTH], zs[:, 2 * WIDTH:3 * WIDTH]
    wd, ad, gd = zls[:, 0:LANES], zls[:, LANES:2 * LANES], zls[:, 2 * LANES:4 * LANES]
    ones_bd = ones_ref[...]

    lw = w0_ref[...] + _dot(jnp.tanh(wd).astype(BF16), ww2_ref[...])
    nlw = -lw
    softplus = jnp.maximum(nlw, 0.0) + jnp.log(1.0 + jnp.exp(-jnp.abs(nlw)))
    logw_ref[...] = -jnp.exp(-softplus - 0.5)
    a = _sigmoid(a0_ref[...] + _dot(ad.astype(BF16), wa2_ref[...]))
    g_ref[...] = _dot(_sigmoid(gd).astype(BF16), wg2_ref[...])
    kkv = k * kk_ref[...]
    norm = jnp.sqrt(_headsum_full(kkv * kkv, ones_bd))
    kap = kkv / jnp.maximum(norm, 1e-12)
    kt = k * (1.0 + (a - 1.0) * ka_ref[...])
    kap_ref[...] = kap
    bb_ref[...] = kap * a
    kt_ref[...] = kt
    r_ref[...] = r
    v_ref[...] = v
    bonus_ref[...] = _headsum_full(r * kt * rk_ref[...], ones_bd) * v


def _prep_call(p_main, prev_rkv, prev_lora, wts, nb, t, tt, carry):
    m = nb * t
    nt = t // tt
    tok = lambda n, cb: pl.BlockSpec((tt, n), lambda b, i: (b * nt + i, cb))
    if carry:
        prev_specs = [pl.BlockSpec((1, 1, 3 * WIDTH), lambda b, i: (b, 0, 0)),
                      pl.BlockSpec((1, 1, 4 * LANES), lambda b, i: (b, 0, 0))]
    else:
        prev_specs = [tok(3 * WIDTH, 0), tok(4 * LANES, 0)]
    full = lambda a: pl.BlockSpec(a.shape, lambda b, i: (0,) * a.ndim)
    out_spec = tok(WIDTH, 0)
    return pl.pallas_call(
        functools.partial(_prep_kernel, carry=carry),
        out_shape=tuple(jax.ShapeDtypeStruct((m, WIDTH), F32) for _ in range(8)),
        grid=(nb, nt),
        in_specs=[tok(3 * WIDTH, 0), tok(4 * LANES, C_LORA // (4 * LANES))] + prev_specs + [full(a) for a in wts],
        out_specs=(out_spec,) * 8,
        scratch_shapes=[pltpu.VMEM((1, 3 * WIDTH), F32), pltpu.VMEM((1, 4 * LANES), F32)],
        compiler_params=_params(("arbitrary", "arbitrary"), 48),
        name="rwkv_prep",
    )(p_main, p_main, prev_rkv, prev_lora, *wts)


def _post_math(y, bonus, g, lnw, lnb, ones_bd):
    mu = _headsum(y, ones_bd) * (1.0 / HEAD_DIM)
    d = y - mu
    var = _headsum(d * d, ones_bd) * (1.0 / HEAD_DIM)
    yn = d * lax.rsqrt(var + GN_EPS) * lnw + lnb
    return ((yn + bonus) * g).astype(BF16)


def _post_kernel(y_ref, bonus_ref, g_ref, lnw_ref, lnb_ref, ones_ref, o_ref):
    for p in range(PAIRS):
        sl = slice(p * LANES, (p + 1) * LANES)
        o_ref[:, sl] = _post_math(y_ref[:, sl], bonus_ref[:, sl], g_ref[:, sl], lnw_ref[:, sl],
                                  lnb_ref[:, sl], ones_ref[...])


def _post_call(y, bonus, g, lnw, lnb, ones_bd):
    return pl.pallas_call(
        _post_kernel,
        out_shape=jax.ShapeDtypeStruct(y.shape, BF16),
        name="rwkv_post",
    )(y, bonus, g, lnw, lnb, ones_bd)


CHUNK = 64


def _chunk_kernel(logw_ref, kap_ref, bb_ref, kt_ref, r_ref, v_ref, bonus_ref, g_ref, lnw_ref, lnb_ref,
                  ones_ref, tri_ref, rw_ref, sfin_ref, s_ref):
    c = pl.program_id(1)

    @pl.when(c == 0)
    def _():
        s_ref[...] = jnp.zeros_like(s_ref)

    lane = lax.broadcasted_iota(I32, (1, LANES), 1)
    m0 = (lane < HEAD_DIM).astype(F32)
    m1 = 1.0 - m0
    rr = lax.broadcasted_iota(I32, (LANES, LANES), 0)
    cc = lax.broadcasted_iota(I32, (LANES, LANES), 1)
    strict = (cc & (CHUNK - 1)) < (rr & (CHUNK - 1))
    incl = (cc & (CHUNK - 1)) <= (rr & (CHUNK - 1))
    eye = (rr == cc).astype(F32)
    ones_bd = ones_ref[...]
    tri = tri_ref[...]

    def stack(x):
        return jnp.concatenate([x * m0, x * m1], axis=0)

    pairs = range(PAIRS)
    sls = [slice(p * LANES, (p + 1) * LANES) for p in pairs]
    n2 = 2 * CHUNK

    lw = logw_ref[...]
    h1 = lw.astype(BF16)
    r1 = lw - h1.astype(F32)
    h2 = r1.astype(BF16)
    h3 = (r1 - h2.astype(F32)).astype(BF16)
    lam = _dot(tri, h1) + (_dot(tri, h2) + _dot(tri, h3))
    lam_c = lam[CHUNK - 1:CHUNK, :]
    e_in = jnp.exp(lam)
    e_ex = jnp.exp(lam - lw)
    e_inv = jnp.exp(-lam)
    e_rem = jnp.exp(lam_c - lam)
    g_c = jnp.exp(lam_c)
    kh, rh = kap_ref[...] * e_ex, r_ref[...] * e_in
    bc, kc = bb_ref[...] * e_inv, kt_ref[...] * e_inv
    bt, ktr = bb_ref[...] * e_rem, kt_ref[...] * e_rem
    s_kh = [stack(kh[:, sl]) for sl in sls]
    s_rh = [stack(rh[:, sl]) for sl in sls]
    s_v = [stack(v_ref[:, sl]).astype(BF16) for sl in sls]
    bt_t = [stack(bt[:, sl]).T.astype(BF16) for sl in sls]
    kt_t = [stack(ktr[:, sl]).T.astype(BF16) for sl in sls]

    lhs = [jnp.concatenate([s_kh[p], s_rh[p]], axis=0).astype(BF16) for p in pairs]
    rhs = [jnp.concatenate([stack(bc[:, sl]), stack(kc[:, sl])], axis=0).astype(BF16) for sl in sls]
    gram = [_dot(lhs[p], rhs[p], _NT) for p in pairs]
    l_k = [jnp.where(strict, g[:n2, n2:], 0.0).astype(BF16) for g in gram]
    a_b = [jnp.where(incl, g[n2:, :n2], 0.0).astype(BF16) for g in gram]
    a_k = [jnp.where(incl, g[n2:, n2:], 0.0).astype(BF16) for g in gram]

    pw = [jnp.where(strict, -g[:n2, :n2], 0.0) for g in gram]
    tinv = [eye + x for x in pw]
    for _ in range(5):
        pwb = [x.astype(BF16) for x in pw]
        pw = [_dot(x, x) for x in pwb]
        tinv = [tinv[p] + _dot(tinv[p].astype(BF16), pw[p].astype(BF16)) for p in pairs]
    tinv = [x.astype(BF16) for x in tinv]

    lkv = [_dot(l_k[p], s_v[p]).astype(BF16) for p in pairs]
    wu = [_dot(tinv[p], jnp.concatenate([s_kh[p].astype(BF16), lkv[p]], axis=1)).astype(BF16) for p in pairs]
    mn = [_dot(bt_t[p], wu[p]) for p in pairs]
    ktv = [_dot(kt_t[p], s_v[p]) for p in pairs]
    ab_wu = [_dot(a_b[p], wu[p]) for p in pairs]
    akv = [_dot(a_k[p], s_v[p]) for p in pairs]
    m_mat = [eye * g_c[:, sls[p]] - mn[p][:, :LANES] for p in pairs]
    step_lhs = [jnp.concatenate([s_rh[p] - ab_wu[p][:, :LANES], m_mat[p]], axis=0).astype(BF16) for p in pairs]
    upd = [_dot(step_lhs[p], s_ref[p].astype(BF16)) for p in pairs]
    for p in pairs:
        s_ref[p] = upd[p][n2:] + (ktv[p] - mn[p][:, LANES:])
    for p in pairs:
        s_y = upd[p][:n2] + (akv[p] - ab_wu[p][:, LANES:])
        y = s_y[0:CHUNK, :] + s_y[CHUNK:2 * CHUNK, :]
        sl = sls[p]
        rw_ref[:, sl] = _post_math(y, bonus_ref[:, sl], g_ref[:, sl], lnw_ref[:, sl], lnb_ref[:, sl], ones_bd)

    @pl.when(c == pl.num_programs(1) - 1)
    def _():
        sfin_ref[0] = s_ref[...]


def _chunk_call(feats, lnw, lnb, ones_bd, tri, nb, t):
    nc = t // CHUNK
    tok = pl.BlockSpec((CHUNK, WIDTH), lambda b, c: (b * nc + c, 0))
    full = lambda a: pl.BlockSpec(a.shape, lambda b, c: (0,) * a.ndim)
    return pl.pallas_call(
        _chunk_kernel,
        out_shape=(jax.ShapeDtypeStruct((nb * t, WIDTH), BF16),
                   jax.ShapeDtypeStruct((nb, PAIRS, LANES, LANES), F32)),
        grid=(nb, nc),
        in_specs=[tok] * 8 + [full(lnw), full(lnb), full(ones_bd), full(tri)],
        out_specs=(tok, pl.BlockSpec((1, PAIRS, LANES, LANES), lambda b, c: (b, 0, 0, 0))),
        scratch_shapes=[pltpu.VMEM((PAIRS, LANES, LANES), F32)],
        compiler_params=_params(("arbitrary", "arbitrary"), 32),
        name="rwkv_chunk_scan",
    )(*feats, lnw, lnb, ones_bd, tri)


def _step_kernel(s_ref, logw_ref, kap_ref, bb_ref, kt_ref, r_ref, v_ref, so_ref, y_ref):
    s = s_ref[0]
    sa = -jnp.sum(s * kap_ref[0], axis=-1, keepdims=True)
    s_new = s * jnp.exp(logw_ref[0]) + sa * bb_ref[0] + v_ref[0] * kt_ref[0]
    so_ref[0] = s_new
    y_ref[0] = jnp.sum(s_new * r_ref[0], axis=-1, keepdims=True)


def _step_call(state, logw, kap, bb, kt, r, v):
    nb = state.shape[0]
    row = lambda x: x.reshape(nb, HEADS, 1, HEAD_DIM)
    s_spec = pl.BlockSpec((1, HEADS, HEAD_DIM, HEAD_DIM), lambda b: (b, 0, 0, 0))
    r_spec = pl.BlockSpec((1, HEADS, 1, HEAD_DIM), lambda b: (b, 0, 0, 0))
    c_spec = pl.BlockSpec((1, HEADS, HEAD_DIM, 1), lambda b: (b, 0, 0, 0))
    return pl.pallas_call(
        _step_kernel,
        out_shape=(jax.ShapeDtypeStruct(state.shape, F32), jax.ShapeDtypeStruct((nb, HEADS, HEAD_DIM, 1), F32)),
        grid=(nb,),
        in_specs=[s_spec] + [r_spec] * 5 + [c_spec],
        out_specs=(s_spec, c_spec),
        compiler_params=_params(("arbitrary",), 32),
        name="rwkv_step",
    )(state, row(logw), row(kap), row(bb), row(kt), row(r), v.reshape(nb, HEADS, HEAD_DIM, 1))


def _sortable(x):
    x = jnp.where(x == 0.0, 0.0, x)
    bits = pltpu.bitcast(x, I32)
    return jnp.where(bits < 0, bits ^ jnp.int32(0x7FFFFFFF), bits)


def _select_topk(keys_ref, nkc, rows, kc_size, col_of, idx_bits):
    def count(pred):
        def body(kc, acc):
            x = jnp.where(pred(kc), 1.0, 0.0)
            part = x[:, 0:LANES]
            for q in range(1, kc_size // LANES):
                part = part + x[:, q * LANES:(q + 1) * LANES]
            return acc + part
        acc = lax.fori_loop(0, nkc, body, jnp.zeros((rows, LANES), F32))
        return jnp.sum(acc, axis=-1, keepdims=True)

    kf = float(TOPK)
    thr = jnp.where(count(lambda kc: keys_ref[kc] >= 0) >= kf, jnp.int32(0), jnp.int32(INT_MIN))

    def bit_body(it, thr):
        cand = thr + lax.shift_left(jnp.int32(1), jnp.int32(30) - it)
        return jnp.where(count(lambda kc: keys_ref[kc] >= cand) >= kf, cand, thr)

    thr = lax.fori_loop(0, 31, bit_body, thr)
    need = kf - count(lambda kc: keys_ref[kc] > thr)
    split = (count(lambda kc: keys_ref[kc] == thr) > need) & (thr > jnp.int32(INT_MIN))
    any_split = jnp.max(jnp.where(split, 1.0, 0.0)) > 0.0

    def tie_break():
        def tie_body(it, x):
            cand = x + lax.shift_left(jnp.int32(1), jnp.int32(idx_bits - 1) - it)
            c = count(lambda kc: (keys_ref[kc] == thr) & (col_of(kc) <= cand))
            return jnp.where(c < need, cand, x)

        return lax.fori_loop(0, idx_bits, tie_body, jnp.full((rows, 1), -1, I32)) + 1

    jb = lax.cond(any_split, tie_break, lambda: jnp.full((rows, 1), INT_MAX, I32))
    return thr, jb


TQ = 128
KC = 512


M_INIT = -1e30


def _idx_key_pack(kidx):
    hi, lo = _split(kidx)
    return jnp.concatenate([hi, lo, hi, jnp.zeros_like(hi)], axis=1)


def _attn_kernel(qidx_ref, widx_ref, kcat_ref, aq_ref, kt_ref, vt_ref,
                 att_ref, keys_ref, bias_ref, qcat_ref, d_ref, qm_ref, m_ref, l_ref, acc_ref):
    i = pl.program_id(1)
    nkc = lax.shift_right_logical((i + 1) * TQ + (KC - 1), int(np.log2(KC)))
    row = i * TQ + lax.broadcasted_iota(I32, (TQ, KC), 0)
    lane_kc = lax.broadcasted_iota(I32, (TQ, KC), 1)
    col_of = lambda kc: kc * KC + lane_kc
    head_rows = lambda h: slice(h * TQ, (h + 1) * TQ)
    first = lax.broadcasted_iota(I32, (1, LANES), 1) < HEAD_DIM

    for p in range(PAIRS):
        x = qidx_ref[:, p * LANES:(p + 1) * LANES]
        hi = x.astype(BF16).astype(F32)
        lo = x - hi
        hi_sw = pltpu.roll(hi, HEAD_DIM, 1)
        lo_sw = pltpu.roll(lo, HEAD_DIM, 1)
        for hh in range(2):
            rows = head_rows(2 * p + hh)
            own_hi, other_hi = (hi, hi_sw) if hh == 0 else (hi_sw, hi)
            qcat_ref[rows, 0:LANES] = jnp.where(first, own_hi, other_hi).astype(BF16)
            qcat_ref[rows, LANES:2 * LANES] = jnp.where(first, lo if hh == 0 else lo_sw, 0.0).astype(BF16)
    wv = widx_ref[:, IDX_DIM:IDX_DIM + HEADS] * (HEADS ** -0.5 * IDX_DIM ** -0.5)

    def score_body(kc, carry):
        d_ref[...] = _dot(qcat_ref[...], kcat_ref[kc])
        acc = jnp.zeros((TQ, KC), F32)
        for h in range(HEADS):
            acc = acc + jnp.maximum(d_ref[head_rows(h), :], 0.0) * wv[:, h:h + 1]
        keys_ref[kc] = jnp.where(col_of(kc) <= row, _sortable(acc), jnp.int32(INT_MIN))
        return carry

    lax.fori_loop(0, nkc, score_body, 0)

    thr, jb = _select_topk(keys_ref, nkc, TQ, KC, col_of, 12)

    def bias_body(kc, carry):
        key = keys_ref[kc]
        col = col_of(kc)
        sel = ((key > thr) | ((key == thr) & (col <= jb))) & (col <= row)
        bias_ref[kc] = jnp.where(sel, 0.0, 2.0 * M_INIT)
        return carry

    lax.fori_loop(0, nkc, bias_body, 0)

    for p in range(PAIRS):
        aq = aq_ref[:, p * LANES:(p + 1) * LANES] * (HEAD_DIM ** -0.5)
        qm_ref[p, 0:TQ, :] = jnp.where(first, aq, 0.0).astype(BF16)
        qm_ref[p, TQ:2 * TQ, :] = jnp.where(first, 0.0, aq).astype(BF16)
    def lane_tiles(x, op):
        out = x[:, 0:LANES]
        for q in range(1, KC // LANES):
            out = op(out, x[:, q * LANES:(q + 1) * LANES])
        return out

    halves = (slice(0, TQ), slice(TQ, 2 * TQ))

    l_ref[...] = jnp.full(l_ref.shape, M_INIT, F32)

    def max_body(kc, carry):
        bias = bias_ref[kc]
        for p in range(PAIRS):
            s = _dot(qm_ref[p], kt_ref[kc, p * LANES:(p + 1) * LANES, :])
            for rs in halves:
                l_ref[p, rs, :] = jnp.maximum(l_ref[p, rs, :], lane_tiles(s[rs] + bias, jnp.maximum))
        return carry

    lax.fori_loop(0, nkc, max_body, 0)
    for p in range(PAIRS):
        m_ref[p] = jnp.max(l_ref[p], axis=-1, keepdims=True)

    l_ref[...] = jnp.zeros(l_ref.shape, F32)
    acc_ref[...] = jnp.zeros(acc_ref.shape, F32)

    def pv_body(kc, carry):
        bias = bias_ref[kc]
        sls = [slice(p * LANES, (p + 1) * LANES) for p in range(PAIRS)]
        s = [_dot(qm_ref[p], kt_ref[kc, sls[p], :]) for p in range(PAIRS)]
        es = []
        for p in range(PAIRS):
            e = [jnp.exp(s[p][rs] + bias - m_ref[p, rs, :]) for rs in halves]
            for rs, x in zip(halves, e):
                l_ref[p, rs, :] += lane_tiles(x, jnp.add)
            es.append(jnp.concatenate([x.astype(BF16) for x in e], axis=0))
        pv = [_dot(es[p], vt_ref[kc, sls[p], :], _NT) for p in range(PAIRS)]
        for p in range(PAIRS):
            acc_ref[p] += pv[p]
        return carry

    lax.fori_loop(0, nkc, pv_body, 0)

    for p in range(PAIRS):
        o = acc_ref[p] / jnp.sum(l_ref[p], axis=-1, keepdims=True)
        att_ref[:, p * LANES:(p + 1) * LANES] = jnp.where(first, o[0:TQ], o[TQ:2 * TQ]).astype(BF16)


def _attn_call(p_idx, kcat_t, p_main, kt, vt, nb, t):
    nq = t // TQ
    nkc = t // KC
    qtok = lambda n, cb: pl.BlockSpec((TQ, n), lambda b, i: (b * nq + i, cb))
    seq3 = lambda n: pl.BlockSpec((nkc, n, KC), lambda b, i: (b, 0, 0), pipeline_mode=pl.Buffered(1))
    return pl.pallas_call(
        _attn_kernel,
        out_shape=jax.ShapeDtypeStruct((nb * t, WIDTH), BF16),
        grid=(nb, nq),
        in_specs=[qtok(WIDTH, 0), qtok(LANES, WIDTH // LANES), seq3(4 * IDX_DIM),
                  qtok(WIDTH, C_AQ // WIDTH), seq3(WIDTH), seq3(WIDTH)],
        out_specs=qtok(WIDTH, 0),
        scratch_shapes=[pltpu.VMEM((nkc, TQ, KC), I32), pltpu.VMEM((nkc, TQ, KC), F32),
                        pltpu.VMEM((HEADS * TQ, 2 * LANES), BF16), pltpu.VMEM((HEADS * TQ, KC), F32),
                        pltpu.VMEM((PAIRS, 2 * TQ, LANES), BF16), pltpu.VMEM((PAIRS, 2 * TQ, 1), F32),
                        pltpu.VMEM((PAIRS, 2 * TQ, LANES), F32), pltpu.VMEM((PAIRS, 2 * TQ, LANES), F32)],
        compiler_params=_params(("arbitrary", "arbitrary"), 48),
        name="dsa_prompt_attention",
    )(p_idx, p_idx, kcat_t, p_main, kt, vt)


PG = 8
SROWS = 72
SCOLS = SROWS * PAGE


def _sidx_kernel(pt_ref, q_ref, w_ref, knew_ref, *rest, npages):
    pages, o_ref = rest[:PG], rest[PG]
    j = pl.program_id(1)
    qh, ql = _split(q_ref[0])
    w = w_ref[0] * (HEADS ** -0.5 * IDX_DIM ** -0.5)

    def score(kpage):
        kh, kl = _split(kpage)
        d = _dot(qh, kh) + (_dot(qh, kl) + _dot(ql, kh))
        return jnp.sum(jnp.maximum(d, 0.0) * w, axis=0, keepdims=True)

    @pl.when(j == 0)
    def _():
        o_ref[0] = jnp.zeros((SROWS, PAGE), F32)
        o_ref[0, npages:npages + 1, :] = score(knew_ref[0])

    ks = [_split(pages[g][0]) for g in range(PG)]
    ds = [_dot(qh, kh) + (_dot(qh, kl) + _dot(ql, kh)) for kh, kl in ks]
    for g in range(PG):
        o_ref[0, pl.ds(j * PG + g, 1), :] = jnp.sum(jnp.maximum(ds[g], 0.0) * w, axis=0, keepdims=True)


def _sidx_call(page_table, q, w, knew, cache_kidx):
    nb, npages = page_table.shape
    one = lambda s: pl.BlockSpec((1,) + s, lambda b, j, pt: (b, 0, 0))
    page = lambda g: pl.BlockSpec((1, IDX_DIM, PAGE), lambda b, j, pt: (pt[b, j * PG + g], 0, 0))
    return pl.pallas_call(
        functools.partial(_sidx_kernel, npages=npages),
        out_shape=jax.ShapeDtypeStruct((nb, SROWS, PAGE), F32),
        grid_spec=pltpu.PrefetchScalarGridSpec(
            num_scalar_prefetch=1,
            grid=(nb, npages // PG),
            in_specs=[one((HEADS, IDX_DIM)), one((HEADS, 1)), one((IDX_DIM, PAGE))] + [page(g) for g in range(PG)],
            out_specs=one((SROWS, PAGE)),
        ),
        compiler_params=_params(("arbitrary", "arbitrary"), 32),
        name="dsa_sample_index_scores",
    )(page_table, q, w, knew, *([cache_kidx] * PG))


SKC = 512
SNKC = SCOLS // SKC


def _ssel_kernel(sc_ref, bias_ref, keys_ref, *, n_valid):
    rows = sc_ref.shape[0]
    lane_kc = lax.broadcasted_iota(I32, (rows, SKC), 1)
    col_of = lambda kc: kc * SKC + lane_kc
    for c in range(SNKC):
        x = sc_ref[:, c * SKC:(c + 1) * SKC]
        keys_ref[c] = jnp.where(col_of(c) < n_valid, _sortable(x), jnp.int32(INT_MIN))
    thr, jb = _select_topk(keys_ref, SNKC, rows, SKC, col_of, 14)
    for c in range(SNKC):
        key = keys_ref[c]
        col = col_of(c)
        sel = ((key > thr) | ((key == thr) & (col <= jb))) & (col < n_valid)
        bias_ref[:, c * SKC:(c + 1) * SKC] = jnp.where(sel, 0.0, NEG)


def _ssel_call(scores, n_valid):
    rows = scores.shape[0]
    return pl.pallas_call(
        functools.partial(_ssel_kernel, n_valid=n_valid),
        out_shape=jax.ShapeDtypeStruct(scores.shape, F32),
        scratch_shapes=[pltpu.VMEM((SNKC, rows, SKC), I32)],
        name="dsa_sample_select",
    )(scores)


def _sattn_kernel(pt_ref, q_ref, kn_ref, vn_ref, bias_ref, *rest, npages):
    kpages, vpages = rest[:PG], rest[PG:2 * PG]
    o_ref, qb_ref, m_ref, l_ref, acc_ref = rest[2 * PG:]
    j = pl.program_id(1)
    scale = HEAD_DIM ** -0.5

    @pl.when(j == 0)
    def _():
        on = bias_ref[0, npages:npages + 1, 0:1] == 0.0
        first = on & (lax.broadcasted_iota(I32, (1, PAGE), 1) == 0)
        for h in range(HEADS):
            qh = q_ref[0, h] * scale
            qb_ref[h] = jnp.broadcast_to(qh, (HEAD_DIM, PAGE))
            s_new = jnp.sum(qh * kn_ref[0, h], axis=0, keepdims=True)
            m_ref[h] = jnp.broadcast_to(jnp.where(on, s_new, NEG), (1, PAGE))
            l_ref[h] = jnp.broadcast_to(jnp.where(on, 1.0, 0.0), (1, PAGE))
            acc_ref[h] = jnp.where(first, vn_ref[0, h], 0.0)

    for g in range(PG):
        on = bias_ref[0, pl.ds(j * PG + g, 1), :] == 0.0
        for h in range(HEADS):
            s = jnp.sum(qb_ref[h] * kpages[g][0, h], axis=0, keepdims=True)
            m = m_ref[h]
            m_new = jnp.maximum(m, jnp.max(jnp.where(on, s, NEG), axis=-1, keepdims=True))
            e = jnp.where(on, jnp.exp(s - m_new), 0.0)
            alpha = jnp.exp(m - m_new)
            m_ref[h] = m_new
            l_ref[h] = alpha * l_ref[h] + jnp.sum(e, axis=-1, keepdims=True)
            acc_ref[h] = alpha * acc_ref[h] + e * vpages[g][0, h]

    @pl.when(j == pl.num_programs(1) - 1)
    def _():
        for h in range(HEADS):
            o_ref[0, h] = jnp.sum(acc_ref[h], axis=-1, keepdims=True) / l_ref[h][:, 0:1]


def _sattn_call(page_table, q, kn, vn, bias, ck, cv):
    nb, npages = page_table.shape
    one = lambda s: pl.BlockSpec((1,) + s, lambda b, j, pt: (b,) + (0,) * len(s))
    col = one((HEADS, HEAD_DIM, 1))
    page = lambda g: pl.BlockSpec((1, HEADS, HEAD_DIM, PAGE), lambda b, j, pt: (pt[b, j * PG + g], 0, 0, 0))
    return pl.pallas_call(
        functools.partial(_sattn_kernel, npages=npages),
        out_shape=jax.ShapeDtypeStruct((nb, HEADS, HEAD_DIM, 1), F32),
        grid_spec=pltpu.PrefetchScalarGridSpec(
            num_scalar_prefetch=1,
            grid=(nb, npages // PG),
            in_specs=[col, col, col, one((SROWS, PAGE))] + [page(g) for g in range(PG)] * 2,
            out_specs=col,
            scratch_shapes=[pltpu.VMEM((HEADS, HEAD_DIM, PAGE), F32), pltpu.VMEM((HEADS, 1, PAGE), F32),
                            pltpu.VMEM((HEADS, 1, PAGE), F32), pltpu.VMEM((HEADS, HEAD_DIM, PAGE), F32)],
        ),
        compiler_params=_params(("arbitrary", "arbitrary"), 40),
        name="dsa_sample_attention",
    )(page_table, q, kn, vn, bias, *([ck] * PG), *([cv] * PG))


def _merge_kernel(rw_ref, att_ref, wr_ref, wa_ref, ga_ref, gb_ref, o_ref):
    yr = _dot(rw_ref[...], wr_ref[...])
    ya = _dot(att_ref[...], wa_ref[...])
    o_ref[...] = (_sigmoid(ga_ref[...]) * yr + _sigmoid(gb_ref[...]) * ya).astype(BF16)


def _merge_call(rw, att, w_r, w_a, p_main, tm, tn):
    m = rw.shape[0]
    lhs = pl.BlockSpec((tm, WIDTH), lambda j, i: (i, 0))
    rhs = pl.BlockSpec((WIDTH, tn), lambda j, i: (0, j))
    gate = lambda c0: pl.BlockSpec((tm, tn), lambda j, i: (i, c0 // tn + j))
    return pl.pallas_call(
        _merge_kernel,
        out_shape=jax.ShapeDtypeStruct((m, D_MODEL), BF16),
        grid=(D_MODEL // tn, m // tm),
        in_specs=[lhs, lhs, rhs, rhs, gate(C_GA), gate(C_GB)],
        out_specs=pl.BlockSpec((tm, tn), lambda j, i: (i, j)),
        compiler_params=_params(("arbitrary", "arbitrary"), 32),
        name="mixer_merge",
    )(rw, att, w_r, w_a, p_main, p_main)


def _rms(x, g):
    return x * lax.rsqrt(jnp.mean(x * x, axis=-1, keepdims=True) + RMS_EPS) * g


def _outproj_kernel(m_ref, w_ref, x_ref, gpost_ref, gpre_ref, h_ref, f_ref):
    h = x_ref[...] + _rms(_dot(m_ref[...], w_ref[...]), gpost_ref[...])
    h_ref[...] = h
    f_ref[...] = _rms(h, gpre_ref[...]).astype(BF16)


def _outproj_call(mix, w_out, x, g_post, g_pre, tm):
    m = x.shape[0]
    tok = pl.BlockSpec((tm, D_MODEL), lambda i: (i, 0))
    full = lambda a: pl.BlockSpec(a.shape, lambda i: (0, 0))
    return pl.pallas_call(
        _outproj_kernel,
        out_shape=(jax.ShapeDtypeStruct((m, D_MODEL), F32), jax.ShapeDtypeStruct((m, D_MODEL), BF16)),
        grid=(m // tm,),
        in_specs=[tok, full(w_out), tok, full(g_post), full(g_pre)],
        out_specs=(tok, tok),
        compiler_params=_params(("arbitrary",), 40),
        name="mixer_out_proj",
    )(mix, w_out, x, g_post, g_pre)


def _ffn_kernel(f_ref, wg_ref, wu_ref, wd_ref, h_ref, gp_ref, y_ref, acc_ref):
    j = pl.program_id(1)
    f = f_ref[...]
    gate = _dot(f, wg_ref[...])
    t = (gate * _sigmoid(gate) * _dot(f, wu_ref[...])).astype(BF16)
    part = _dot(t, wd_ref[...])

    @pl.when(j == 0)
    def _():
        acc_ref[...] = part

    @pl.when(j > 0)
    def _():
        acc_ref[...] += part

    @pl.when(j == pl.num_programs(1) - 1)
    def _():
        y_ref[...] = h_ref[...] + _rms(acc_ref[...], gp_ref[...])


def _ffn_call(f, w_gate, w_up, w_down, h, g_post, tm, tf):
    m = f.shape[0]
    d_ff = w_gate.shape[1]
    tok = pl.BlockSpec((tm, D_MODEL), lambda i, j: (i, 0))
    return pl.pallas_call(
        _ffn_kernel,
        out_shape=jax.ShapeDtypeStruct((m, D_MODEL), F32),
        grid=(m // tm, d_ff // tf),
        in_specs=[tok, pl.BlockSpec((D_MODEL, tf), lambda i, j: (0, j)), pl.BlockSpec((D_MODEL, tf), lambda i, j: (0, j)),
                  pl.BlockSpec((tf, D_MODEL), lambda i, j: (j, 0)), tok, pl.BlockSpec((1, D_MODEL), lambda i, j: (0, 0))],
        out_specs=tok,
        scratch_shapes=[pltpu.VMEM((tm, D_MODEL), F32)],
        compiler_params=_params(("arbitrary", "arbitrary"), 48),
        name="swiglu_ffn",
    )(f, w_gate, w_up, w_down, h, g_post)


def _pad_cols(a, n):
    return jnp.pad(a, ((0, 0), (0, n - a.shape[1])))


def _regroup_cols(a):
    w3, ls = 3 * WIDTH, LORA_SMALL
    lora = jnp.concatenate([_pad_cols(a[:, w3:w3 + ls], LANES), _pad_cols(a[:, w3 + ls:w3 + 2 * ls], LANES),
                            a[:, w3 + 2 * ls:]], axis=1)
    return a[:, :w3], lora


def _ungroup_cols(rkv, lora):
    ls = LORA_SMALL
    return jnp.concatenate([rkv, lora[:, :ls], lora[:, LANES:LANES + ls], lora[:, 2 * LANES:]], axis=1)


def kernel(x_prompt, x_sample, cache_k, cache_v, cache_kidx, state_wkv, state_shift, page_table, g_pre_mix, w_in,
           rwkv_mu, w0, w_w2, a0, w_a2, w_g2, k_k, k_a, r_k, ln_x_w, ln_x_b, w_o_rwkv, w_o_attn, w_out,
           g_post_mix, g_pre_ffn, w_gate, w_up, w_down, g_post_ffn):
    nb, t, _ = x_prompt.shape
    ns = x_sample.shape[0]
    row = lambda a: a.reshape(1, -1).astype(F32)

    o_q = RWKV_COLS
    o_qi = o_q + 3 * WIDTH
    o_ki = o_qi + WIDTH
    o_ga = o_ki + IDX_DIM + HEADS
    w_rkv, w_lora = _regroup_cols(w_in[:, :RWKV_COLS])
    o_k, o_v = o_q + WIDTH, o_q + 2 * WIDTH
    w_main = jnp.concatenate([w_rkv, w_in[:, o_q:o_k], w_in[:, o_ga:], w_lora], axis=1).astype(BF16)
    w_kv = w_in[:, o_k:o_qi].astype(BF16)
    w_in_t = w_in.T
    w_kt, w_vt = w_in_t[o_k:o_v].astype(BF16), w_in_t[o_v:o_qi].astype(BF16)
    w_idx_hi, w_idx_lo = _split(jnp.concatenate([w_in[:, o_qi:o_ki], _pad_cols(w_in[:, o_ki:o_ga], LANES)], axis=1))
    mu_rkv, mu_lora = _regroup_cols(row(rwkv_mu))
    pad_rows = lambda a: jnp.pad(a, ((0, LANES - a.shape[0]), (0, 0))).astype(BF16)
    lane = np.arange(LANES)
    ones_bd = jnp.asarray((lane[:, None] // HEAD_DIM) == (lane[None, :] // HEAD_DIM), BF16)
    tri = jnp.asarray(np.arange(CHUNK)[:, None] >= np.arange(CHUNK)[None, :], BF16)
    prep_w = [mu_rkv, mu_lora, row(w0), row(a0), row(k_k), row(k_a), row(r_k), pad_rows(w_w2), pad_rows(w_a2),
              w_g2.astype(BF16), ones_bd]
    lnw, lnb = row(ln_x_w), row(ln_x_b)
    w_or, w_oa, w_o = w_o_rwkv.astype(BF16), w_o_attn.astype(BF16), w_out.astype(BF16)
    w_g, w_u, w_d = w_gate.astype(BF16), w_up.astype(BF16), w_down.astype(BF16)
    gains = [row(g_pre_mix), row(g_post_mix), row(g_pre_ffn), row(g_post_ffn)]

    def project(x2d, tm, transposed):
        u = _rms_call(x2d, gains[0], min(tm, 512), transposed)
        p_main = _mm_call([u[0]], [w_main], [(0, 0)], F32, tm, 512, "in_proj")
        p_idx = _mm_call([u[0], u[1]], [w_idx_hi, w_idx_lo], [(0, 0), (0, 1), (1, 0)], F32, tm, 384, "in_proj_idx")
        return p_main, p_idx, u

    def finish(x2d, p_main, rw, att, tm, tn, tm_ffn):
        mix = _merge_call(rw, att, w_or, w_oa, p_main, tm, tn)
        h, f = _outproj_call(mix, w_o, x2d, gains[1], gains[2], min(tm, 256))
        return _ffn_call(f, w_g, w_u, w_d, h, gains[3], tm_ffn, 512)

    m = nb * t
    xp = x_prompt.reshape(m, D_MODEL)
    p_main, p_idx, u_p = project(xp, 1024, True)
    zero_rkv = jnp.zeros((nb, 1, 3 * WIDTH), F32)
    zero_lora = jnp.zeros((nb, 1, 4 * LANES), F32)
    feats = _prep_call(p_main, zero_rkv, zero_lora, prep_w, nb, t, 256, True)
    rw, s_fin = _chunk_call(feats, lnw, lnb, ones_bd, tri, nb, t)
    kidx = p_idx[:, WIDTH:WIDTH + IDX_DIM]
    k_t, k_tc = _proj_t_call(w_kt, u_p[2], nb, t, KC)
    v_t, v_tc = _proj_t_call(w_vt, u_p[2], nb, t, KC)
    chunk_t = lambda a: jnp.swapaxes(a.reshape(m // KC, KC, a.shape[1]), 1, 2)
    att = _attn_call(p_idx, chunk_t(_idx_key_pack(kidx)), p_main, k_tc, v_tc, nb, t)
    tok_major = lambda a: jnp.transpose(a.reshape(nb, HEADS, HEAD_DIM, t), (0, 3, 1, 2))
    k_p, v_p = tok_major(k_t), tok_major(v_t)
    y_p = finish(xp, p_main, rw, att, 1024, 512, 512)
    blocks = jnp.stack([s_fin[:, :, :HEAD_DIM, :HEAD_DIM], s_fin[:, :, HEAD_DIM:, HEAD_DIM:]], axis=2)
    wkv_p = jnp.swapaxes(blocks.reshape(nb, HEADS, HEAD_DIM, HEAD_DIM), -1, -2)
    last = p_main.reshape(nb, t, N_MAIN)[:, -1]
    shift_p = _ungroup_cols(last[:, :3 * WIDTH], last[:, C_LORA:])

    xs = x_sample.reshape(ns, D_MODEL)
    ps_main, ps_idx, u_s = project(xs, ns, False)
    ps_kv = _mm_call([u_s[0]], [w_kv], [(0, 0)], F32, ns, 512, "in_proj_kv")
    prev_rkv, prev_lora = _regroup_cols(state_shift)
    sf = _prep_call(ps_main, prev_rkv, prev_lora, prep_w, 1, ns, ns, False)
    logw_s, kap_s, bb_s, kt_s, r_s, v_s, bonus_s, g_s = sf
    wkv_s, y_s = _step_call(state_wkv, logw_s, kap_s, bb_s, kt_s, r_s, v_s)
    rw_s = _post_call(y_s.reshape(ns, WIDTH), bonus_s, g_s, lnw, lnb, ones_bd)

    npages = page_table.shape[1]
    past = npages * PAGE
    q_i = ps_idx[:, :WIDTH].reshape(ns, HEADS, IDX_DIM)
    w_i = ps_idx[:, WIDTH + IDX_DIM:WIDTH + IDX_DIM + HEADS].reshape(ns, HEADS, 1)
    kidx_s = ps_idx[:, WIDTH:WIDTH + IDX_DIM]
    knew = jnp.pad(kidx_s[:, :, None], ((0, 0), (0, 0), (0, PAGE - 1)))
    scores = _sidx_call(page_table, q_i, w_i, knew, jnp.transpose(cache_kidx, (0, 2, 1)))
    bias = _ssel_call(scores.reshape(ns, SCOLS), past + 1).reshape(ns, SROWS, PAGE)
    k_s, v_s_att = ps_kv[:, :WIDTH], ps_kv[:, WIDTH:]
    col = lambda a: a.reshape(ns, HEADS, HEAD_DIM, 1)
    att_s = _sattn_call(page_table, col(ps_main[:, C_AQ:C_AQ + WIDTH]), col(k_s), col(v_s_att), bias,
                        jnp.transpose(cache_k, (0, 2, 3, 1)), jnp.transpose(cache_v, (0, 2, 3, 1)))
    y_smp = finish(xs, ps_main, rw_s, att_s.reshape(ns, WIDTH).astype(BF16), ns, 512, ns)
    shift_s = _ungroup_cols(ps_main[:, :3 * WIDTH], ps_main[:, C_LORA:])

    hd = (HEADS, HEAD_DIM)
    return (y_p.reshape(nb, t, D_MODEL), y_smp.reshape(ns, 1, D_MODEL),
            k_p, v_p, kidx.reshape(nb, t, IDX_DIM),
            wkv_p, shift_p,
            k_s.reshape(ns, 1, *hd), v_s_att.reshape(ns, 1, *hd), kidx_s.reshape(ns, 1, IDX_DIM),
            wkv_s, shift_s)
```

```python
import functools

import numpy as np
import jax
import jax.numpy as jnp
from jax import lax
from jax.experimental import pallas as pl
from jax.experimental.pallas import tpu as pltpu

F32, BF16, I32 = jnp.float32, jnp.bfloat16, jnp.int32

D_MODEL = 2048
HEADS = 16
HEAD_DIM = 64
WIDTH = HEADS * HEAD_DIM
LORA_SMALL = 96
LORA_GATE = 256
RWKV_COLS = 3 * WIDTH + 2 * LORA_SMALL + LORA_GATE
IDX_DIM = 64
TOPK = 256
PAGE = 128
RMS_EPS = 1e-6
GN_EPS = 64e-5

LANES = 128
VMEM_BYTES = 64 * 1024 * 1024
PAIRS = WIDTH // LANES

C_R, C_K, C_V, C_AQ, C_GA, C_GB, C_LORA = 0, 1024, 2048, 3072, 4096, 6144, 8192
N_MAIN = 8704
N_IDX = 1152

INT_MIN = int(np.iinfo(np.int32).min)
INT_MAX = int(np.iinfo(np.int32).max)
NEG = -1e30

_NN = (((1,), (0,)), ((), ()))
_NT = (((1,), (1,)), ((), ()))


def _params(sem, vmem_mb):
    return pltpu.CompilerParams(dimension_semantics=sem, vmem_limit_bytes=vmem_mb * 1024 * 1024)


def _dot(a, b, dims=_NN):
    return lax.dot_general(a, b, dims, preferred_element_type=F32)


def _split(x):
    hi = x.astype(BF16)
    return hi, (x - hi.astype(F32)).astype(BF16)


def _mm3(ap, bp, dims=_NN):
    return _dot(ap[0], bp[0], dims) + (_dot(ap[0], bp[1], dims) + _dot(ap[1], bp[0], dims))


def _sigmoid(x):
    return 1.0 / (1.0 + jnp.exp(-x))


def _headsum(x, ones_bd):
    hi, lo = _split(x)
    return _dot(hi, ones_bd) + _dot(lo, ones_bd)


def _headsum_full(x, ones_bd):
    return jnp.concatenate(
        [_headsum(x[:, p * LANES:(p + 1) * LANES], ones_bd) for p in range(PAIRS)], axis=1)


def _rms_kernel(x_ref, g_ref, hi_ref, lo_ref, *t_ref):
    x = x_ref[...]
    y = x * lax.rsqrt(jnp.mean(x * x, axis=-1, keepdims=True) + RMS_EPS) * g_ref[...]
    hi = y.astype(BF16)
    hi_ref[...] = hi
    lo_ref[...] = (y - hi.astype(F32)).astype(BF16)
    if t_ref:
        t_ref[0][...] = hi.astype(F32).T.astype(BF16)


def _rms_call(x, g, tm, transposed=False):
    m, d = x.shape
    spec = pl.BlockSpec((tm, d), lambda i: (i, 0))
    shapes = [jax.ShapeDtypeStruct((m, d), BF16), jax.ShapeDtypeStruct((m, d), BF16)]
    specs = [spec, spec]
    if transposed:
        shapes.append(jax.ShapeDtypeStruct((d, m), BF16))
        specs.append(pl.BlockSpec((d, tm), lambda i: (0, i)))
    return pl.pallas_call(
        _rms_kernel,
        out_shape=tuple(shapes),
        grid=(m // tm,),
        in_specs=[spec, pl.BlockSpec((1, d), lambda i: (0, 0))],
        out_specs=tuple(specs),
        compiler_params=_params(("arbitrary",), 40),
        name="rms_norm",
    )(x, g)


def _proj_t_kernel(w_ref, ut_ref, o32_ref, o16_ref):
    y = _dot(w_ref[...], ut_ref[...])
    o32_ref[0] = y
    o16_ref[0] = y.astype(BF16)


def _proj_t_call(w_t, u_t, nb, t, tn):
    n, d = w_t.shape
    nc = t // tn
    return pl.pallas_call(
        _proj_t_kernel,
        out_shape=(jax.ShapeDtypeStruct((nb, n, t), F32), jax.ShapeDtypeStruct((nb * nc, n, tn), BF16)),
        grid=(nb, nc),
        in_specs=[pl.BlockSpec((n, d), lambda b, j: (0, 0)), pl.BlockSpec((d, tn), lambda b, j: (0, b * nc + j))],
        out_specs=(pl.BlockSpec((1, n, tn), lambda b, j: (b, 0, j)),
                   pl.BlockSpec((1, n, tn), lambda b, j: (b * nc + j, 0, 0))),
        compiler_params=_params(("arbitrary", "arbitrary"), 40),
        name="in_proj_t",
    )(w_t, u_t)


def _mm_kernel(*refs, na, nb, terms):
    a, b, o = refs[:na], refs[na:na + nb], refs[na + nb]
    acc = None
    for ia, ib in terms:
        d = _dot(a[ia][...], b[ib][...])
        acc = d if acc is None else acc + d
    o[...] = acc.astype(o.dtype)


def _mm_call(a_list, b_list, terms, out_dtype, tm, tn, name):
    m, k = a_list[0].shape
    n = b_list[0].shape[1]
    a_spec = pl.BlockSpec((tm, k), lambda j, i: (i, 0))
    b_spec = pl.BlockSpec((k, tn), lambda j, i: (0, j))
    return pl.pallas_call(
        functools.partial(_mm_kernel, na=len(a_list), nb=len(b_list), terms=terms),
        out_shape=jax.ShapeDtypeStruct((m, n), out_dtype),
        grid=(n // tn, m // tm),
        in_specs=[a_spec] * len(a_list) + [b_spec] * len(b_list),
        out_specs=pl.BlockSpec((tm, tn), lambda j, i: (i, j)),
        compiler_params=_params(("arbitrary", "arbitrary"), 48),
        name=name,
    )(*a_list, *b_list)


def _shift_rows(z, first_row):
    rows = lax.broadcasted_iota(I32, z.shape, 0)
    return jnp.where(rows == 0, first_row, pltpu.roll(z, 1, 0))


N_PREP_W = 11


def _prep_math(z, zl, zp, zlp, w):
    mu_rkv, mu_lora, w0, a0, k_k, k_a, r_k, w_w2, w_a2, w_g2, ones = [x[...] for x in w]
    zs = z + (zp - z) * mu_rkv
    zls = zl + (zlp - zl) * mu_lora
    r, k, v = zs[:, 0:WIDTH], zs[:, WIDTH:2 * WIDTH], zs[:, 2 * WIDTH:3 * WIDTH]
    wd, ad, gd = zls[:, 0:LANES], zls[:, LANES:2 * LANES], zls[:, 2 * LANES:4 * LANES]
    nlw = -(w0 + _dot(jnp.tanh(wd).astype(BF16), w_w2))
    softplus = jnp.maximum(nlw, 0.0) + jnp.log(1.0 + jnp.exp(-jnp.abs(nlw)))
    logw = -jnp.exp(-softplus - 0.5)
    a = _sigmoid(a0 + _dot(ad.astype(BF16), w_a2))
    g = _dot(_sigmoid(gd).astype(BF16), w_g2)
    kkv = k * k_k
    kap = kkv / jnp.maximum(jnp.sqrt(_headsum_full(kkv * kkv, ones)), 1e-12)
    kt = k * (1.0 + (a - 1.0) * k_a)
    bonus = _headsum_full(r * kt * r_k, ones) * v
    return logw, kap, kap * a, kt, r, v, bonus, g


def _prep_kernel(rkv_ref, lora_ref, p_rkv_ref, p_lora_ref, *rest):
    w, outs = rest[:N_PREP_W], rest[N_PREP_W:]
    vals = _prep_math(rkv_ref[...], lora_ref[...], p_rkv_ref[...], p_lora_ref[...], w)
    for o, x in zip(outs, vals):
        o[...] = x


def _prep_call(p_main, prev_rkv, prev_lora, wts):
    n = p_main.shape[0]
    tok = lambda c, cb: pl.BlockSpec((n, c), lambda i: (0, cb))
    full = lambda a: pl.BlockSpec(a.shape, lambda i: (0,) * a.ndim)
    return pl.pallas_call(
        _prep_kernel,
        out_shape=tuple(jax.ShapeDtypeStruct((n, WIDTH), F32) for _ in range(8)),
        grid=(1,),
        in_specs=[tok(3 * WIDTH, 0), tok(4 * LANES, C_LORA // (4 * LANES)), tok(3 * WIDTH, 0), tok(4 * LANES, 0)]
        + [full(a) for a in wts],
        out_specs=(tok(WIDTH, 0),) * 8,
        compiler_params=_params(("arbitrary",), 32),
        name="rwkv_prep",
    )(p_main, p_main, prev_rkv, prev_lora, *wts)


def _post_math(y, bonus, g, lnw, lnb, ones_bd):
    mu = _headsum(y, ones_bd) * (1.0 / HEAD_DIM)
    d = y - mu
    var = _headsum(d * d, ones_bd) * (1.0 / HEAD_DIM)
    yn = d * lax.rsqrt(var + GN_EPS) * lnw + lnb
    return ((yn + bonus) * g).astype(BF16)


def _post_kernel(y_ref, bonus_ref, g_ref, lnw_ref, lnb_ref, ones_ref, o_ref):
    for p in range(PAIRS):
        sl = slice(p * LANES, (p + 1) * LANES)
        o_ref[:, sl] = _post_math(y_ref[:, sl], bonus_ref[:, sl], g_ref[:, sl], lnw_ref[:, sl],
                                  lnb_ref[:, sl], ones_ref[...])


def _post_call(y, bonus, g, lnw, lnb, ones_bd):
    return pl.pallas_call(
        _post_kernel,
        out_shape=jax.ShapeDtypeStruct(y.shape, BF16),
        name="rwkv_post",
    )(y, bonus, g, lnw, lnb, ones_bd)


CHUNK = 64


def _chunk_kernel(rkv_ref, lora_ref, *rest):
    prep_w = rest[:N_PREP_W]
    lnw_ref, lnb_ref, tri_ref, rw_ref, sfin_ref, s_ref, c_rkv, c_lora = rest[N_PREP_W:]
    ones_ref = prep_w[-1]
    c = pl.program_id(1)

    @pl.when(c == 0)
    def _():
        s_ref[...] = jnp.zeros_like(s_ref)
        c_rkv[...] = jnp.zeros_like(c_rkv)
        c_lora[...] = jnp.zeros_like(c_lora)

    z, zl = rkv_ref[...], lora_ref[...]
    zp, zlp = _shift_rows(z, c_rkv[...]), _shift_rows(zl, c_lora[...])
    c_rkv[...] = z[CHUNK - 1:CHUNK, :]
    c_lora[...] = zl[CHUNK - 1:CHUNK, :]
    lw, kap, bb, kt, r, v, bonus, g = _prep_math(z, zl, zp, zlp, prep_w)

    lane = lax.broadcasted_iota(I32, (1, LANES), 1)
    m0 = (lane < HEAD_DIM).astype(F32)
    m1 = 1.0 - m0
    rr = lax.broadcasted_iota(I32, (LANES, LANES), 0)
    cc = lax.broadcasted_iota(I32, (LANES, LANES), 1)
    strict = (cc & (CHUNK - 1)) < (rr & (CHUNK - 1))
    incl = (cc & (CHUNK - 1)) <= (rr & (CHUNK - 1))
    eye = (rr == cc).astype(F32)
    ones_bd = ones_ref[...]
    tri = tri_ref[...]

    def stack(x):
        return jnp.concatenate([x * m0, x * m1], axis=0)

    pairs = range(PAIRS)
    sls = [slice(p * LANES, (p + 1) * LANES) for p in pairs]
    n2 = 2 * CHUNK

    h1 = lw.astype(BF16)
    r1 = lw - h1.astype(F32)
    h2 = r1.astype(BF16)
    h3 = (r1 - h2.astype(F32)).astype(BF16)
    lam = _dot(tri, h1) + (_dot(tri, h2) + _dot(tri, h3))
    lam_c = lam[CHUNK - 1:CHUNK, :]
    e_in = jnp.exp(lam)
    e_ex = jnp.exp(lam - lw)
    e_inv = jnp.exp(-lam)
    e_rem = jnp.exp(lam_c - lam)
    g_c = jnp.exp(lam_c)
    kh, rh = kap * e_ex, r * e_in
    bc, kc = bb * e_inv, kt * e_inv
    bt, ktr = bb * e_rem, kt * e_rem
    s_kh = [stack(kh[:, sl]) for sl in sls]
    s_rh = [stack(rh[:, sl]) for sl in sls]
    s_v = [stack(v[:, sl]).astype(BF16) for sl in sls]
    bt_t = [stack(bt[:, sl]).T.astype(BF16) for sl in sls]
    kt_t = [stack(ktr[:, sl]).T.astype(BF16) for sl in sls]

    lhs = [jnp.concatenate([s_kh[p], s_rh[p]], axis=0).astype(BF16) for p in pairs]
    rhs = [jnp.concatenate([stack(bc[:, sl]), stack(kc[:, sl])], axis=0).astype(BF16) for sl in sls]
    gram = [_dot(lhs[p], rhs[p], _NT) for p in pairs]
    l_k = [jnp.where(strict, g[:n2, n2:], 0.0).astype(BF16) for g in gram]
    a_b = [jnp.where(incl, g[n2:, :n2], 0.0).astype(BF16) for g in gram]
    a_k = [jnp.where(incl, g[n2:, n2:], 0.0).astype(BF16) for g in gram]

    pw = [jnp.where(strict, -g[:n2, :n2], 0.0) for g in gram]
    tinv = [eye + x for x in pw]
    for _ in range(5):
        pwb = [x.astype(BF16) for x in pw]
        pw = [_dot(x, x) for x in pwb]
        tinv = [tinv[p] + _dot(tinv[p].astype(BF16), pw[p].astype(BF16)) for p in pairs]
    tinv = [x.astype(BF16) for x in tinv]

    lkv = [_dot(l_k[p], s_v[p]).astype(BF16) for p in pairs]
    wu = [_dot(tinv[p], jnp.concatenate([s_kh[p].astype(BF16), lkv[p]], axis=1)).astype(BF16) for p in pairs]
    mn = [_dot(bt_t[p], wu[p]) for p in pairs]
    ktv = [_dot(kt_t[p], s_v[p]) for p in pairs]
    ab_wu = [_dot(a_b[p], wu[p]) for p in pairs]
    akv = [_dot(a_k[p], s_v[p]) for p in pairs]
    m_mat = [eye * g_c[:, sls[p]] - mn[p][:, :LANES] for p in pairs]
    step_lhs = [jnp.concatenate([s_rh[p] - ab_wu[p][:, :LANES], m_mat[p]], axis=0).astype(BF16) for p in pairs]
    upd = [_dot(step_lhs[p], s_ref[p].astype(BF16)) for p in pairs]
    for p in pairs:
        s_ref[p] = upd[p][n2:] + (ktv[p] - mn[p][:, LANES:])
    for p in pairs:
        s_y = upd[p][:n2] + (akv[p] - ab_wu[p][:, LANES:])
        y = s_y[0:CHUNK, :] + s_y[CHUNK:2 * CHUNK, :]
        sl = sls[p]
        rw_ref[:, sl] = _post_math(y, bonus[:, sl], g[:, sl], lnw_ref[:, sl], lnb_ref[:, sl], ones_bd)

    @pl.when(c == pl.num_programs(1) - 1)
    def _():
        sfin_ref[0] = s_ref[...]


def _chunk_call(p_main, prep_w, lnw, lnb, tri, nb, t):
    nc = t // CHUNK
    tok = lambda n, cb: pl.BlockSpec((CHUNK, n), lambda b, c: (b * nc + c, cb))
    full = lambda a: pl.BlockSpec(a.shape, lambda b, c: (0,) * a.ndim)
    return pl.pallas_call(
        _chunk_kernel,
        out_shape=(jax.ShapeDtypeStruct((nb * t, WIDTH), BF16),
                   jax.ShapeDtypeStruct((nb, PAIRS, LANES, LANES), F32)),
        grid=(nb, nc),
        in_specs=[tok(3 * WIDTH, 0), tok(4 * LANES, C_LORA // (4 * LANES))]
        + [full(a) for a in prep_w] + [full(lnw), full(lnb), full(tri)],
        out_specs=(tok(WIDTH, 0), pl.BlockSpec((1, PAIRS, LANES, LANES), lambda b, c: (b, 0, 0, 0))),
        scratch_shapes=[pltpu.VMEM((PAIRS, LANES, LANES), F32), pltpu.VMEM((1, 3 * WIDTH), F32),
                        pltpu.VMEM((1, 4 * LANES), F32)],
        compiler_params=_params(("arbitrary", "arbitrary"), 48),
        name="rwkv_chunk_scan",
    )(p_main, p_main, *prep_w, lnw, lnb, tri)


def _step_kernel(s_ref, logw_ref, kap_ref, bb_ref, kt_ref, r_ref, v_ref, so_ref, y_ref):
    s = s_ref[0]
    sa = -jnp.sum(s * kap_ref[0], axis=-1, keepdims=True)
    s_new = s * jnp.exp(logw_ref[0]) + sa * bb_ref[0] + v_ref[0] * kt_ref[0]
    so_ref[0] = s_new
    y_ref[0] = jnp.sum(s_new * r_ref[0], axis=-1, keepdims=True)


def _step_call(state, logw, kap, bb, kt, r, v):
    nb = state.shape[0]
    row = lambda x: x.reshape(nb, HEADS, 1, HEAD_DIM)
    s_spec = pl.BlockSpec((1, HEADS, HEAD_DIM, HEAD_DIM), lambda b: (b, 0, 0, 0))
    r_spec = pl.BlockSpec((1, HEADS, 1, HEAD_DIM), lambda b: (b, 0, 0, 0))
    c_spec = pl.BlockSpec((1, HEADS, HEAD_DIM, 1), lambda b: (b, 0, 0, 0))
    return pl.pallas_call(
        _step_kernel,
        out_shape=(jax.ShapeDtypeStruct(state.shape, F32), jax.ShapeDtypeStruct((nb, HEADS, HEAD_DIM, 1), F32)),
        grid=(nb,),
        in_specs=[s_spec] + [r_spec] * 5 + [c_spec],
        out_specs=(s_spec, c_spec),
        compiler_params=_params(("arbitrary",), 32),
        name="rwkv_step",
    )(state, row(logw), row(kap), row(bb), row(kt), row(r), v.reshape(nb, HEADS, HEAD_DIM, 1))


def _sortable(x):
    x = jnp.where(x == 0.0, 0.0, x)
    bits = pltpu.bitcast(x, I32)
    return jnp.where(bits < 0, bits ^ jnp.int32(0x7FFFFFFF), bits)


def _select_topk(keys_ref, nkc, rows, kc_size, col_of, idx_bits):
    def count(pred):
        def body(kc, acc):
            x = jnp.where(pred(kc), 1.0, 0.0)
            part = x[:, 0:LANES]
            for q in range(1, kc_size // LANES):
                part = part + x[:, q * LANES:(q + 1) * LANES]
            return acc + part
        acc = lax.fori_loop(0, nkc, body, jnp.zeros((rows, LANES), F32))
        return jnp.sum(acc, axis=-1, keepdims=True)

    kf = float(TOPK)
    thr = jnp.where(count(lambda kc: keys_ref[kc] >= 0) >= kf, jnp.int32(0), jnp.int32(INT_MIN))

    def bit_body(it, thr):
        cand = thr + lax.shift_left(jnp.int32(1), jnp.int32(30) - it)
        return jnp.where(count(lambda kc: keys_ref[kc] >= cand) >= kf, cand, thr)

    thr = lax.fori_loop(0, 31, bit_body, thr)
    need = kf - count(lambda kc: keys_ref[kc] > thr)
    split = (count(lambda kc: keys_ref[kc] == thr) > need) & (thr > jnp.int32(INT_MIN))
    any_split = jnp.max(jnp.where(split, 1.0, 0.0)) > 0.0

    def tie_break():
        def tie_body(it, x):
            cand = x + lax.shift_left(jnp.int32(1), jnp.int32(idx_bits - 1) - it)
            c = count(lambda kc: (keys_ref[kc] == thr) & (col_of(kc) <= cand))
            return jnp.where(c < need, cand, x)

        return lax.fori_loop(0, idx_bits, tie_body, jnp.full((rows, 1), -1, I32)) + 1

    jb = lax.cond(any_split, tie_break, lambda: jnp.full((rows, 1), INT_MAX, I32))
    return thr, jb


TQ = 128
KC = 512


M_INIT = -1e30


def _idx_key_pack(kidx):
    hi, lo = _split(kidx)
    return jnp.concatenate([hi, lo, hi, jnp.zeros_like(hi)], axis=1)


def _attn_kernel(qidx_ref, widx_ref, kcat_ref, aq_ref, kt_ref, vt_ref,
                 att_ref, keys_ref, bias_ref, qcat_ref, d_ref, qm_ref, m_ref, l_ref, acc_ref):
    i = pl.program_id(1)
    nkc = lax.shift_right_logical((i + 1) * TQ + (KC - 1), int(np.log2(KC)))
    row = i * TQ + lax.broadcasted_iota(I32, (TQ, KC), 0)
    lane_kc = lax.broadcasted_iota(I32, (TQ, KC), 1)
    col_of = lambda kc: kc * KC + lane_kc
    head_rows = lambda h: slice(h * TQ, (h + 1) * TQ)
    first = lax.broadcasted_iota(I32, (1, LANES), 1) < HEAD_DIM

    for p in range(PAIRS):
        x = qidx_ref[:, p * LANES:(p + 1) * LANES]
        hi = x.astype(BF16).astype(F32)
        lo = x - hi
        hi_sw = pltpu.roll(hi, HEAD_DIM, 1)
        lo_sw = pltpu.roll(lo, HEAD_DIM, 1)
        for hh in range(2):
            rows = head_rows(2 * p + hh)
            own_hi, other_hi = (hi, hi_sw) if hh == 0 else (hi_sw, hi)
            qcat_ref[rows, 0:LANES] = jnp.where(first, own_hi, other_hi).astype(BF16)
            qcat_ref[rows, LANES:2 * LANES] = jnp.where(first, lo if hh == 0 else lo_sw, 0.0).astype(BF16)
    wv = widx_ref[:, IDX_DIM:IDX_DIM + HEADS] * (HEADS ** -0.5 * IDX_DIM ** -0.5)

    def score_body(kc, carry):
        d_ref[...] = _dot(qcat_ref[...], kcat_ref[kc])
        acc = jnp.zeros((TQ, KC), F32)
        for h in range(HEADS):
            acc = acc + jnp.maximum(d_ref[head_rows(h), :], 0.0) * wv[:, h:h + 1]
        keys_ref[kc] = jnp.where(col_of(kc) <= row, _sortable(acc), jnp.int32(INT_MIN))
        return carry

    lax.fori_loop(0, nkc, score_body, 0)

    thr, jb = _select_topk(keys_ref, nkc, TQ, KC, col_of, 12)

    def bias_body(kc, carry):
        key = keys_ref[kc]
        col = col_of(kc)
        sel = ((key > thr) | ((key == thr) & (col <= jb))) & (col <= row)
        bias_ref[kc] = jnp.where(sel, 0.0, 2.0 * M_INIT)
        return carry

    lax.fori_loop(0, nkc, bias_body, 0)

    for p in range(PAIRS):
        aq = aq_ref[:, p * LANES:(p + 1) * LANES] * (HEAD_DIM ** -0.5)
        qm_ref[p, 0:TQ, :] = jnp.where(first, aq, 0.0).astype(BF16)
        qm_ref[p, TQ:2 * TQ, :] = jnp.where(first, 0.0, aq).astype(BF16)
    def lane_tiles(x, op):
        out = x[:, 0:LANES]
        for q in range(1, KC // LANES):
            out = op(out, x[:, q * LANES:(q + 1) * LANES])
        return out

    halves = (slice(0, TQ), slice(TQ, 2 * TQ))

    m_ref[...] = jnp.full(m_ref.shape, M_INIT, F32)
    l_ref[...] = jnp.zeros(l_ref.shape, F32)
    acc_ref[...] = jnp.zeros(acc_ref.shape, F32)

    def pv_body(kc, carry):
        bias = bias_ref[kc]
        sls = [slice(p * LANES, (p + 1) * LANES) for p in range(PAIRS)]
        s = [_dot(qm_ref[p], kt_ref[kc, sls[p], :]) for p in range(PAIRS)]
        es = []
        for p in range(PAIRS):
            e = []
            for rs in halves:
                sh = s[p][rs] + bias
                m_old = m_ref[p, rs, :]
                m_new = jnp.maximum(m_old, jnp.max(lane_tiles(sh, jnp.maximum), axis=-1, keepdims=True))
                alpha = jnp.exp(m_old - m_new)
                x = jnp.exp(sh - m_new)
                m_ref[p, rs, :] = m_new
                l_ref[p, rs, :] = alpha * l_ref[p, rs, :] + lane_tiles(x, jnp.add)
                acc_ref[p, rs, :] = alpha * acc_ref[p, rs, :]
                e.append(x.astype(BF16))
            es.append(jnp.concatenate(e, axis=0))
        pv = [_dot(es[p], vt_ref[kc, sls[p], :], _NT) for p in range(PAIRS)]
        for p in range(PAIRS):
            acc_ref[p] += pv[p]
        return carry

    lax.fori_loop(0, nkc, pv_body, 0)

    for p in range(PAIRS):
        o = acc_ref[p] / jnp.sum(l_ref[p], axis=-1, keepdims=True)
        att_ref[:, p * LANES:(p + 1) * LANES] = jnp.where(first, o[0:TQ], o[TQ:2 * TQ]).astype(BF16)


def _attn_call(p_idx, kcat_t, p_main, kt, vt, nb, t):
    nq = t // TQ
    nkc = t // KC
    qtok = lambda n, cb: pl.BlockSpec((TQ, n), lambda b, i: (b * nq + i, cb))
    seq3 = lambda n: pl.BlockSpec((nkc, n, KC), lambda b, i: (b, 0, 0), pipeline_mode=pl.Buffered(1))
    return pl.pallas_call(
        _attn_kernel,
        out_shape=jax.ShapeDtypeStruct((nb * t, WIDTH), BF16),
        grid=(nb, nq),
        in_specs=[qtok(WIDTH, 0), qtok(LANES, WIDTH // LANES), seq3(4 * IDX_DIM),
                  qtok(WIDTH, C_AQ // WIDTH), seq3(WIDTH), seq3(WIDTH)],
        out_specs=qtok(WIDTH, 0),
        scratch_shapes=[pltpu.VMEM((nkc, TQ, KC), I32), pltpu.VMEM((nkc, TQ, KC), F32),
                        pltpu.VMEM((HEADS * TQ, 2 * LANES), BF16), pltpu.VMEM((HEADS * TQ, KC), F32),
                        pltpu.VMEM((PAIRS, 2 * TQ, LANES), BF16), pltpu.VMEM((PAIRS, 2 * TQ, 1), F32),
                        pltpu.VMEM((PAIRS, 2 * TQ, LANES), F32), pltpu.VMEM((PAIRS, 2 * TQ, LANES), F32)],
        compiler_params=_params(("arbitrary", "arbitrary"), 48),
        name="dsa_prompt_attention",
    )(p_idx, p_idx, kcat_t, p_main, kt, vt)


PG = 8
SROWS = 72
SCOLS = SROWS * PAGE


PG_IDX = 16


def _sidx_kernel(pt_ref, q_ref, w_ref, knew_ref, *rest, npages):
    pages, o_ref = rest[:PG_IDX], rest[PG_IDX]
    j = pl.program_id(1)
    qh, ql = _split(q_ref[0])
    w = w_ref[0] * (HEADS ** -0.5 * IDX_DIM ** -0.5)

    def score(keys):
        kh, kl = _split(keys)
        d = _dot(qh, kh) + (_dot(qh, kl) + _dot(ql, kh))
        return jnp.sum(jnp.maximum(d, 0.0) * w, axis=0, keepdims=True)

    @pl.when(j == 0)
    def _():
        o_ref[0] = jnp.zeros((SROWS, PAGE), F32)
        o_ref[0, npages:npages + 1, :] = score(knew_ref[0])

    sc = score(jnp.concatenate([pages[g][0] for g in range(PG_IDX)], axis=1))
    for g in range(PG_IDX):
        o_ref[0, pl.ds(j * PG_IDX + g, 1), :] = sc[:, g * PAGE:(g + 1) * PAGE]


def _sidx_call(page_table, q, w, knew, cache_kidx):
    nb, npages = page_table.shape
    one = lambda s: pl.BlockSpec((1,) + s, lambda b, j, pt: (b, 0, 0))
    page = lambda g: pl.BlockSpec((1, IDX_DIM, PAGE), lambda b, j, pt: (pt[b, j * PG_IDX + g], 0, 0))
    return pl.pallas_call(
        functools.partial(_sidx_kernel, npages=npages),
        out_shape=jax.ShapeDtypeStruct((nb, SROWS, PAGE), F32),
        grid_spec=pltpu.PrefetchScalarGridSpec(
            num_scalar_prefetch=1,
            grid=(nb, npages // PG_IDX),
            in_specs=[one((HEADS, IDX_DIM)), one((HEADS, 1)), one((IDX_DIM, PAGE))]
            + [page(g) for g in range(PG_IDX)],
            out_specs=one((SROWS, PAGE)),
        ),
        compiler_params=_params(("arbitrary", "arbitrary"), 32),
        name="dsa_sample_index_scores",
    )(page_table, q, w, knew, *([cache_kidx] * PG_IDX))


SKC = 512
SNKC = SCOLS // SKC


def _ssel_kernel(sc_ref, bias_ref, keys_ref, *, n_valid):
    rows = sc_ref.shape[0]
    lane_kc = lax.broadcasted_iota(I32, (rows, SKC), 1)
    col_of = lambda kc: kc * SKC + lane_kc
    for c in range(SNKC):
        x = sc_ref[:, c * SKC:(c + 1) * SKC]
        keys_ref[c] = jnp.where(col_of(c) < n_valid, _sortable(x), jnp.int32(INT_MIN))
    thr, jb = _select_topk(keys_ref, SNKC, rows, SKC, col_of, 14)
    for c in range(SNKC):
        key = keys_ref[c]
        col = col_of(c)
        sel = ((key > thr) | ((key == thr) & (col <= jb))) & (col < n_valid)
        bias_ref[:, c * SKC:(c + 1) * SKC] = jnp.where(sel, 0.0, NEG)


def _ssel_call(scores, n_valid):
    rows = scores.shape[0]
    return pl.pallas_call(
        functools.partial(_ssel_kernel, n_valid=n_valid),
        out_shape=jax.ShapeDtypeStruct(scores.shape, F32),
        scratch_shapes=[pltpu.VMEM((SNKC, rows, SKC), I32)],
        name="dsa_sample_select",
    )(scores)


def _sattn_kernel(pt_ref, q_ref, kn_ref, vn_ref, bias_ref, *rest, npages):
    kpages, vpages = rest[:PG], rest[PG:2 * PG]
    o_ref, qb_ref, m_ref, l_ref, acc_ref = rest[2 * PG:]
    j = pl.program_id(1)
    scale = HEAD_DIM ** -0.5

    @pl.when(j == 0)
    def _():
        on = bias_ref[0, npages:npages + 1, 0:1] == 0.0
        first = on & (lax.broadcasted_iota(I32, (1, PAGE), 1) == 0)
        for h in range(HEADS):
            qh = q_ref[0, h] * scale
            qb_ref[h] = jnp.broadcast_to(qh, (HEAD_DIM, PAGE))
            s_new = jnp.sum(qh * kn_ref[0, h], axis=0, keepdims=True)
            m_ref[h] = jnp.broadcast_to(jnp.where(on, s_new, NEG), (1, PAGE))
            l_ref[h] = jnp.broadcast_to(jnp.where(on, 1.0, 0.0), (1, PAGE))
            acc_ref[h] = jnp.where(first, vn_ref[0, h], 0.0)

    for g in range(PG):
        on = bias_ref[0, pl.ds(j * PG + g, 1), :] == 0.0
        for h in range(HEADS):
            s = jnp.sum(qb_ref[h] * kpages[g][0, h], axis=0, keepdims=True)
            m = m_ref[h]
            m_new = jnp.maximum(m, jnp.max(jnp.where(on, s, NEG), axis=-1, keepdims=True))
            e = jnp.where(on, jnp.exp(s - m_new), 0.0)
            alpha = jnp.exp(m - m_new)
            m_ref[h] = m_new
            l_ref[h] = alpha * l_ref[h] + jnp.sum(e, axis=-1, keepdims=True)
            acc_ref[h] = alpha * acc_ref[h] + e * vpages[g][0, h]

    @pl.when(j == pl.num_programs(1) - 1)
    def _():
        for h in range(HEADS):
            o_ref[0, h] = jnp.sum(acc_ref[h], axis=-1, keepdims=True) / l_ref[h][:, 0:1]


def _sattn_call(page_table, q, kn, vn, bias, ck, cv):
    nb, npages = page_table.shape
    one = lambda s: pl.BlockSpec((1,) + s, lambda b, j, pt: (b,) + (0,) * len(s))
    col = one((HEADS, HEAD_DIM, 1))
    page = lambda g: pl.BlockSpec((1, HEADS, HEAD_DIM, PAGE), lambda b, j, pt: (pt[b, j * PG + g], 0, 0, 0))
    return pl.pallas_call(
        functools.partial(_sattn_kernel, npages=npages),
        out_shape=jax.ShapeDtypeStruct((nb, HEADS, HEAD_DIM, 1), F32),
        grid_spec=pltpu.PrefetchScalarGridSpec(
            num_scalar_prefetch=1,
            grid=(nb, npages // PG),
            in_specs=[col, col, col, one((SROWS, PAGE))] + [page(g) for g in range(PG)] * 2,
            out_specs=col,
            scratch_shapes=[pltpu.VMEM((HEADS, HEAD_DIM, PAGE), F32), pltpu.VMEM((HEADS, 1, PAGE), F32),
                            pltpu.VMEM((HEADS, 1, PAGE), F32), pltpu.VMEM((HEADS, HEAD_DIM, PAGE), F32)],
        ),
        compiler_params=_params(("arbitrary", "arbitrary"), 40),
        name="dsa_sample_attention",
    )(page_table, q, kn, vn, bias, *([ck] * PG), *([cv] * PG))


def _merge_kernel(rw_ref, att_ref, wr_ref, wa_ref, ga_ref, gb_ref, o_ref):
    yr = _dot(rw_ref[...], wr_ref[...])
    ya = _dot(att_ref[...], wa_ref[...])
    o_ref[...] = (_sigmoid(ga_ref[...]) * yr + _sigmoid(gb_ref[...]) * ya).astype(BF16)


def _merge_call(rw, att, w_r, w_a, p_main, tm, tn):
    m = rw.shape[0]
    lhs = pl.BlockSpec((tm, WIDTH), lambda j, i: (i, 0))
    rhs = pl.BlockSpec((WIDTH, tn), lambda j, i: (0, j))
    gate = lambda c0: pl.BlockSpec((tm, tn), lambda j, i: (i, c0 // tn + j))
    return pl.pallas_call(
        _merge_kernel,
        out_shape=jax.ShapeDtypeStruct((m, D_MODEL), BF16),
        grid=(D_MODEL // tn, m // tm),
        in_specs=[lhs, lhs, rhs, rhs, gate(C_GA), gate(C_GB)],
        out_specs=pl.BlockSpec((tm, tn), lambda j, i: (i, j)),
        compiler_params=_params(("arbitrary", "arbitrary"), 32),
        name="mixer_merge",
    )(rw, att, w_r, w_a, p_main, p_main)


def _rms(x, g):
    return x * lax.rsqrt(jnp.mean(x * x, axis=-1, keepdims=True) + RMS_EPS) * g


def _outproj_kernel(m_ref, w_ref, x_ref, gpost_ref, gpre_ref, h_ref, f_ref):
    h = x_ref[...] + _rms(_dot(m_ref[...], w_ref[...]), gpost_ref[...])
    h_ref[...] = h
    f_ref[...] = _rms(h, gpre_ref[...]).astype(BF16)


def _outproj_call(mix, w_out, x, g_post, g_pre, tm):
    m = x.shape[0]
    tok = pl.BlockSpec((tm, D_MODEL), lambda i: (i, 0))
    full = lambda a: pl.BlockSpec(a.shape, lambda i: (0, 0))
    return pl.pallas_call(
        _outproj_kernel,
        out_shape=(jax.ShapeDtypeStruct((m, D_MODEL), F32), jax.ShapeDtypeStruct((m, D_MODEL), BF16)),
        grid=(m // tm,),
        in_specs=[tok, full(w_out), tok, full(g_post), full(g_pre)],
        out_specs=(tok, tok),
        compiler_params=_params(("arbitrary",), 40),
        name="mixer_out_proj",
    )(mix, w_out, x, g_post, g_pre)


def _ffn_kernel(f_ref, wg_ref, wu_ref, wd_ref, h_ref, gp_ref, y_ref, acc_ref):
    j = pl.program_id(1)

    @pl.when(j == 0)
    def _():
        acc_ref[...] = jnp.zeros(acc_ref.shape, F32)

    f = f_ref[...]
    gate = _dot(f, wg_ref[...])
    t = (gate * _sigmoid(gate) * _dot(f, wu_ref[...])).astype(BF16)
    acc_ref[...] += _dot(t, wd_ref[...])

    @pl.when(j == pl.num_programs(1) - 1)
    def _():
        y_ref[...] = h_ref[...] + _rms(acc_ref[...], gp_ref[...])


def _ffn_call(f, w_gate, w_up, w_down, h, g_post, tm, tf):
    m = f.shape[0]
    d_ff = w_gate.shape[1]
    tok = pl.BlockSpec((tm, D_MODEL), lambda i, j: (i, 0))
    return pl.pallas_call(
        _ffn_kernel,
        out_shape=jax.ShapeDtypeStruct((m, D_MODEL), F32),
        grid=(m // tm, d_ff // tf),
        in_specs=[tok, pl.BlockSpec((D_MODEL, tf), lambda i, j: (0, j)), pl.BlockSpec((D_MODEL, tf), lambda i, j: (0, j)),
                  pl.BlockSpec((tf, D_MODEL), lambda i, j: (j, 0)), tok, pl.BlockSpec((1, D_MODEL), lambda i, j: (0, 0))],
        out_specs=tok,
        scratch_shapes=[pltpu.VMEM((tm, D_MODEL), F32)],
        compiler_params=_params(("arbitrary", "arbitrary"), 48),
        name="swiglu_ffn",
    )(f, w_gate, w_up, w_down, h, g_post)


def _pad_cols(a, n):
    return jnp.pad(a, ((0, 0), (0, n - a.shape[1])))


def _regroup_cols(a):
    w3, ls = 3 * WIDTH, LORA_SMALL
    lora = jnp.concatenate([_pad_cols(a[:, w3:w3 + ls], LANES), _pad_cols(a[:, w3 + ls:w3 + 2 * ls], LANES),
                            a[:, w3 + 2 * ls:]], axis=1)
    return a[:, :w3], lora


def _ungroup_cols(rkv, lora):
    ls = LORA_SMALL
    return jnp.concatenate([rkv, lora[:, :ls], lora[:, LANES:LANES + ls], lora[:, 2 * LANES:]], axis=1)


def kernel(x_prompt, x_sample, cache_k, cache_v, cache_kidx, state_wkv, state_shift, page_table, g_pre_mix, w_in,
           rwkv_mu, w0, w_w2, a0, w_a2, w_g2, k_k, k_a, r_k, ln_x_w, ln_x_b, w_o_rwkv, w_o_attn, w_out,
           g_post_mix, g_pre_ffn, w_gate, w_up, w_down, g_post_ffn):
    nb, t, _ = x_prompt.shape
    ns = x_sample.shape[0]
    row = lambda a: a.reshape(1, -1).astype(F32)

    o_q = RWKV_COLS
    o_qi = o_q + 3 * WIDTH
    o_ki = o_qi + WIDTH
    o_ga = o_ki + IDX_DIM + HEADS
    w_rkv, w_lora = _regroup_cols(w_in[:, :RWKV_COLS])
    o_k, o_v = o_q + WIDTH, o_q + 2 * WIDTH
    w_main = jnp.concatenate([w_rkv, w_in[:, o_q:o_k], w_in[:, o_ga:], w_lora], axis=1).astype(BF16)
    w_kv = w_in[:, o_k:o_qi].astype(BF16)
    w_kt, w_vt = w_kv[:, :WIDTH].T, w_kv[:, WIDTH:].T
    w_idx_hi, w_idx_lo = _split(jnp.concatenate([w_in[:, o_qi:o_ki], _pad_cols(w_in[:, o_ki:o_ga], LANES)], axis=1))
    mu_rkv, mu_lora = _regroup_cols(row(rwkv_mu))
    pad_rows = lambda a: jnp.pad(a, ((0, LANES - a.shape[0]), (0, 0))).astype(BF16)
    lane = np.arange(LANES)
    ones_bd = jnp.asarray((lane[:, None] // HEAD_DIM) == (lane[None, :] // HEAD_DIM), BF16)
    tri = jnp.asarray(np.arange(CHUNK)[:, None] >= np.arange(CHUNK)[None, :], BF16)
    prep_w = [mu_rkv, mu_lora, row(w0), row(a0), row(k_k), row(k_a), row(r_k), pad_rows(w_w2), pad_rows(w_a2),
              w_g2.astype(BF16), ones_bd]
    lnw, lnb = row(ln_x_w), row(ln_x_b)
    w_or, w_oa, w_o = w_o_rwkv.astype(BF16), w_o_attn.astype(BF16), w_out.astype(BF16)
    w_g, w_u, w_d = w_gate.astype(BF16), w_up.astype(BF16), w_down.astype(BF16)
    gains = [row(g_pre_mix), row(g_post_mix), row(g_pre_ffn), row(g_post_ffn)]

    def project(x2d, tm, transposed):
        u = _rms_call(x2d, gains[0], min(tm, 512), transposed)
        p_main = _mm_call([u[0]], [w_main], [(0, 0)], F32, min(tm, 512), N_MAIN // 4, "in_proj")
        p_idx = _mm_call([u[0], u[1]], [w_idx_hi, w_idx_lo], [(0, 0), (0, 1), (1, 0)], F32, tm, 384, "in_proj_idx")
        return p_main, p_idx, u

    def finish(x2d, p_main, rw, att, tm, tn, tm_ffn):
        mix = _merge_call(rw, att, w_or, w_oa, p_main, tm, tn)
        h, f = _outproj_call(mix, w_o, x2d, gains[1], gains[2], min(tm, 256))
        return _ffn_call(f, w_g, w_u, w_d, h, gains[3], tm_ffn, 512)

    m = nb * t
    xp = x_prompt.reshape(m, D_MODEL)
    p_main, p_idx, u_p = project(xp, 1024, True)
    rw, s_fin = _chunk_call(p_main, prep_w, lnw, lnb, tri, nb, t)
    kidx = p_idx[:, WIDTH:WIDTH + IDX_DIM]
    k_t, k_tc = _proj_t_call(w_kt, u_p[2], nb, t, KC)
    v_t, v_tc = _proj_t_call(w_vt, u_p[2], nb, t, KC)
    chunk_t = lambda a: jnp.swapaxes(a.reshape(m // KC, KC, a.shape[1]), 1, 2)
    att = _attn_call(p_idx, chunk_t(_idx_key_pack(kidx)), p_main, k_tc, v_tc, nb, t)
    tok_major = lambda a: jnp.transpose(a.reshape(nb, HEADS, HEAD_DIM, t), (0, 3, 1, 2))
    k_p, v_p = tok_major(k_t), tok_major(v_t)
    y_p = finish(xp, p_main, rw, att, 1024, 512, 512)
    blocks = jnp.stack([s_fin[:, :, :HEAD_DIM, :HEAD_DIM], s_fin[:, :, HEAD_DIM:, HEAD_DIM:]], axis=2)
    wkv_p = jnp.swapaxes(blocks.reshape(nb, HEADS, HEAD_DIM, HEAD_DIM), -1, -2)
    last = p_main.reshape(nb, t, N_MAIN)[:, -1]
    shift_p = _ungroup_cols(last[:, :3 * WIDTH], last[:, C_LORA:])

    xs = x_sample.reshape(ns, D_MODEL)
    ps_main, ps_idx, u_s = project(xs, ns, False)
    ps_kv = _mm_call([u_s[0]], [w_kv], [(0, 0)], F32, ns, 512, "in_proj_kv")
    prev_rkv, prev_lora = _regroup_cols(state_shift)
    sf = _prep_call(ps_main, prev_rkv, prev_lora, lax.optimization_barrier(prep_w))
    logw_s, kap_s, bb_s, kt_s, r_s, v_s, bonus_s, g_s = sf
    wkv_s, y_s = _step_call(state_wkv, logw_s, kap_s, bb_s, kt_s, r_s, v_s)
    rw_s = _post_call(y_s.reshape(ns, WIDTH), bonus_s, g_s, lnw, lnb, ones_bd)

    npages = page_table.shape[1]
    past = npages * PAGE
    q_i = ps_idx[:, :WIDTH].reshape(ns, HEADS, IDX_DIM)
    w_i = ps_idx[:, WIDTH + IDX_DIM:WIDTH + IDX_DIM + HEADS].reshape(ns, HEADS, 1)
    kidx_s = ps_idx[:, WIDTH:WIDTH + IDX_DIM]
    knew = jnp.pad(kidx_s[:, :, None], ((0, 0), (0, 0), (0, PAGE - 1)))
    scores = _sidx_call(page_table, q_i, w_i, knew, jnp.transpose(cache_kidx, (0, 2, 1)))
    bias = _ssel_call(scores.reshape(ns, SCOLS), past + 1).reshape(ns, SROWS, PAGE)
    k_s, v_s_att = ps_kv[:, :WIDTH], ps_kv[:, WIDTH:]
    col = lambda a: a.reshape(ns, HEADS, HEAD_DIM, 1)
    att_s = _sattn_call(page_table, col(ps_main[:, C_AQ:C_AQ + WIDTH]), col(k_s), col(v_s_att), bias,
                        jnp.transpose(cache_k, (0, 2, 3, 1)), jnp.transpose(cache_v, (0, 2, 3, 1)))
    y_smp = finish(xs, ps_main, rw_s, att_s.reshape(ns, WIDTH).astype(BF16), ns, 512, ns)
    shift_s = _ungroup_cols(ps_main[:, :3 * WIDTH], ps_main[:, C_LORA:])

    hd = (HEADS, HEAD_DIM)
    return (y_p.reshape(nb, t, D_MODEL), y_smp.reshape(ns, 1, D_MODEL),
            k_p, v_p, kidx.reshape(nb, t, IDX_DIM),
            wkv_p, shift_p,
            k_s.reshape(ns, 1, *hd), v_s_att.reshape(ns, 1, *hd), kidx_s.reshape(ns, 1, IDX_DIM),
            wkv_s, shift_s)
```

```python
import functools

import numpy as np
import jax
import jax.numpy as jnp
from jax import lax
from jax.experimental import pallas as pl
from jax.experimental.pallas import tpu as pltpu

F32, BF16, I32 = jnp.float32, jnp.bfloat16, jnp.int32

D_MODEL = 2048
HEADS = 16
HEAD_DIM = 64
WIDTH = HEADS * HEAD_DIM
LORA_SMALL = 96
LORA_GATE = 256
RWKV_COLS = 3 * WIDTH + 2 * LORA_SMALL + LORA_GATE
IDX_DIM = 64
TOPK = 256
PAGE = 128
RMS_EPS = 1e-6
GN_EPS = 64e-5

LANES = 128
PAIRS = WIDTH // LANES

C_AQ, C_GA, C_GB, C_LORA = 3 * WIDTH, 4 * WIDTH, 4 * WIDTH + D_MODEL, 4 * WIDTH + 2 * D_MODEL
N_MAIN = C_LORA + 4 * LANES

INT_MIN = int(np.iinfo(np.int32).min)
INT_MAX = int(np.iinfo(np.int32).max)
NEG = -1e30

_NN = (((1,), (0,)), ((), ()))
_NT = (((1,), (1,)), ((), ()))


def _params(sem, vmem_mb):
    return pltpu.CompilerParams(dimension_semantics=sem, vmem_limit_bytes=vmem_mb * 1024 * 1024)


def _dot(a, b, dims=_NN):
    return lax.dot_general(a, b, dims, preferred_element_type=F32)


def _split(x):
    hi = x.astype(BF16)
    return hi, (x - hi.astype(F32)).astype(BF16)


def _sigmoid(x):
    return 1.0 / (1.0 + jnp.exp(-x))


def _headsum(x, ones_bd):
    hi, lo = _split(x)
    return _dot(hi, ones_bd) + _dot(lo, ones_bd)


def _headsum_full(x, ones_bd):
    return jnp.concatenate(
        [_headsum(x[:, p * LANES:(p + 1) * LANES], ones_bd) for p in range(PAIRS)], axis=1)


def _rms_kernel(x_ref, g_ref, hi_ref, lo_ref, *t_ref):
    x = x_ref[...]
    y = x * lax.rsqrt(jnp.mean(x * x, axis=-1, keepdims=True) + RMS_EPS) * g_ref[...]
    hi = y.astype(BF16)
    hi_ref[...] = hi
    lo_ref[...] = (y - hi.astype(F32)).astype(BF16)
    if t_ref:
        t_ref[0][...] = hi.astype(F32).T.astype(BF16)


def _rms_call(x, g, tm, transposed=False):
    m, d = x.shape
    spec = pl.BlockSpec((tm, d), lambda i: (i, 0))
    shapes = [jax.ShapeDtypeStruct((m, d), BF16), jax.ShapeDtypeStruct((m, d), BF16)]
    specs = [spec, spec]
    if transposed:
        shapes.append(jax.ShapeDtypeStruct((d, m), BF16))
        specs.append(pl.BlockSpec((d, tm), lambda i: (0, i)))
    return pl.pallas_call(
        _rms_kernel,
        out_shape=tuple(shapes),
        grid=(m // tm,),
        in_specs=[spec, pl.BlockSpec((1, d), lambda i: (0, 0))],
        out_specs=tuple(specs),
        compiler_params=_params(("arbitrary",), 40),
        name="rms_norm",
    )(x, g)


def _proj_t_kernel(w_ref, ut_ref, o32_ref, o16_ref):
    y = _dot(w_ref[...], ut_ref[...])
    o32_ref[0] = y
    o16_ref[0] = y.astype(BF16)


def _proj_t_call(w_t, u_t, nb, t, tn):
    n, d = w_t.shape
    nc = t // tn
    return pl.pallas_call(
        _proj_t_kernel,
        out_shape=(jax.ShapeDtypeStruct((nb, n, t), F32), jax.ShapeDtypeStruct((nb * nc, n, tn), BF16)),
        grid=(nb, nc),
        in_specs=[pl.BlockSpec((n, d), lambda b, j: (0, 0)), pl.BlockSpec((d, tn), lambda b, j: (0, b * nc + j))],
        out_specs=(pl.BlockSpec((1, n, tn), lambda b, j: (b, 0, j)),
                   pl.BlockSpec((1, n, tn), lambda b, j: (b * nc + j, 0, 0))),
        compiler_params=_params(("arbitrary", "arbitrary"), 40),
        name="in_proj_t",
    )(w_t, u_t)


def _mm_kernel(*refs, na, nb, terms):
    a, b, o = refs[:na], refs[na:na + nb], refs[na + nb]
    acc = None
    for ia, ib in terms:
        d = _dot(a[ia][...], b[ib][...])
        acc = d if acc is None else acc + d
    o[...] = acc.astype(o.dtype)


def _mm_call(a_list, b_list, terms, out_dtype, tm, tn, name):
    m, k = a_list[0].shape
    n = b_list[0].shape[1]
    a_spec = pl.BlockSpec((tm, k), lambda j, i: (i, 0))
    b_spec = pl.BlockSpec((k, tn), lambda j, i: (0, j))
    return pl.pallas_call(
        functools.partial(_mm_kernel, na=len(a_list), nb=len(b_list), terms=terms),
        out_shape=jax.ShapeDtypeStruct((m, n), out_dtype),
        grid=(n // tn, m // tm),
        in_specs=[a_spec] * len(a_list) + [b_spec] * len(b_list),
        out_specs=pl.BlockSpec((tm, tn), lambda j, i: (i, j)),
        compiler_params=_params(("arbitrary", "arbitrary"), 48),
        name=name,
    )(*a_list, *b_list)


def _shift_rows(z, first_row):
    rows = lax.broadcasted_iota(I32, z.shape, 0)
    return jnp.where(rows == 0, first_row, pltpu.roll(z, 1, 0))


N_PREP_W = 11


def _prep_math(z, zl, zp, zlp, w):
    mu_rkv, mu_lora, w0, a0, k_k, k_a, r_k, w_w2, w_a2, w_g2, ones = [x[...] for x in w]
    zs = z + (zp - z) * mu_rkv
    zls = zl + (zlp - zl) * mu_lora
    r, k, v = zs[:, 0:WIDTH], zs[:, WIDTH:2 * WIDTH], zs[:, 2 * WIDTH:3 * WIDTH]
    wd, ad, gd = zls[:, 0:LANES], zls[:, LANES:2 * LANES], zls[:, 2 * LANES:4 * LANES]
    nlw = -(w0 + _dot(jnp.tanh(wd).astype(BF16), w_w2))
    softplus = jnp.maximum(nlw, 0.0) + jnp.log(1.0 + jnp.exp(-jnp.abs(nlw)))
    logw = -jnp.exp(-softplus - 0.5)
    a = _sigmoid(a0 + _dot(ad.astype(BF16), w_a2))
    g = _dot(_sigmoid(gd).astype(BF16), w_g2)
    kkv = k * k_k
    kap = kkv / jnp.maximum(jnp.sqrt(_headsum_full(kkv * kkv, ones)), 1e-12)
    kt = k * (1.0 + (a - 1.0) * k_a)
    bonus = _headsum_full(r * kt * r_k, ones) * v
    return logw, kap, kap * a, kt, r, v, bonus, g


def _prep_kernel(rkv_ref, lora_ref, p_rkv_ref, p_lora_ref, *rest):
    w, outs = rest[:N_PREP_W], rest[N_PREP_W:]
    vals = _prep_math(rkv_ref[...], lora_ref[...], p_rkv_ref[...], p_lora_ref[...], w)
    for o, x in zip(outs, vals):
        o[...] = x


def _prep_call(p_main, prev_rkv, prev_lora, wts):
    n = p_main.shape[0]
    tok = lambda c, cb: pl.BlockSpec((n, c), lambda i: (0, cb))
    full = lambda a: pl.BlockSpec(a.shape, lambda i: (0,) * a.ndim)
    return pl.pallas_call(
        _prep_kernel,
        out_shape=tuple(jax.ShapeDtypeStruct((n, WIDTH), F32) for _ in range(8)),
        grid=(1,),
        in_specs=[tok(3 * WIDTH, 0), tok(4 * LANES, C_LORA // (4 * LANES)), tok(3 * WIDTH, 0), tok(4 * LANES, 0)]
        + [full(a) for a in wts],
        out_specs=(tok(WIDTH, 0),) * 8,
        compiler_params=_params(("arbitrary",), 32),
        name="rwkv_prep",
    )(p_main, p_main, prev_rkv, prev_lora, *wts)


def _post_math(y, bonus, g, lnw, lnb, ones_bd):
    mu = _headsum(y, ones_bd) * (1.0 / HEAD_DIM)
    d = y - mu
    var = _headsum(d * d, ones_bd) * (1.0 / HEAD_DIM)
    yn = d * lax.rsqrt(var + GN_EPS) * lnw + lnb
    return ((yn + bonus) * g).astype(BF16)


def _post_kernel(y_ref, bonus_ref, g_ref, lnw_ref, lnb_ref, ones_ref, o_ref):
    for p in range(PAIRS):
        sl = slice(p * LANES, (p + 1) * LANES)
        o_ref[:, sl] = _post_math(y_ref[:, sl], bonus_ref[:, sl], g_ref[:, sl], lnw_ref[:, sl],
                                  lnb_ref[:, sl], ones_ref[...])


def _post_call(y, bonus, g, lnw, lnb, ones_bd):
    return pl.pallas_call(
        _post_kernel,
        out_shape=jax.ShapeDtypeStruct(y.shape, BF16),
        name="rwkv_post",
    )(y, bonus, g, lnw, lnb, ones_bd)


CHUNK = 64


def _chunk_kernel(rkv_ref, lora_ref, *rest):
    prep_w = rest[:N_PREP_W]
    lnw_ref, lnb_ref, tri_ref, rw_ref, sfin_ref, s_ref, c_rkv, c_lora = rest[N_PREP_W:]
    ones_ref = prep_w[-1]
    c = pl.program_id(1)

    @pl.when(c == 0)
    def _():
        s_ref[...] = jnp.zeros_like(s_ref)
        c_rkv[...] = jnp.zeros_like(c_rkv)
        c_lora[...] = jnp.zeros_like(c_lora)

    z, zl = rkv_ref[...], lora_ref[...]
    zp, zlp = _shift_rows(z, c_rkv[...]), _shift_rows(zl, c_lora[...])
    c_rkv[...] = z[CHUNK - 1:CHUNK, :]
    c_lora[...] = zl[CHUNK - 1:CHUNK, :]
    lw, kap, bb, kt, r, v, bonus, g = _prep_math(z, zl, zp, zlp, prep_w)

    lane = lax.broadcasted_iota(I32, (1, LANES), 1)
    m0 = (lane < HEAD_DIM).astype(F32)
    m1 = 1.0 - m0
    rr = lax.broadcasted_iota(I32, (LANES, LANES), 0)
    cc = lax.broadcasted_iota(I32, (LANES, LANES), 1)
    strict = (cc & (CHUNK - 1)) < (rr & (CHUNK - 1))
    incl = (cc & (CHUNK - 1)) <= (rr & (CHUNK - 1))
    eye = (rr == cc).astype(F32)
    ones_bd = ones_ref[...]
    tri = tri_ref[...]

    def stack(x):
        return jnp.concatenate([x * m0, x * m1], axis=0)

    pairs = range(PAIRS)
    sls = [slice(p * LANES, (p + 1) * LANES) for p in pairs]
    n2 = 2 * CHUNK

    h1 = lw.astype(BF16)
    r1 = lw - h1.astype(F32)
    h2 = r1.astype(BF16)
    h3 = (r1 - h2.astype(F32)).astype(BF16)
    lam = _dot(tri, h1) + (_dot(tri, h2) + _dot(tri, h3))
    lam_c = lam[CHUNK - 1:CHUNK, :]
    e_in = jnp.exp(lam)
    e_ex = jnp.exp(lam - lw)
    e_inv = jnp.exp(-lam)
    e_rem = jnp.exp(lam_c - lam)
    g_c = jnp.exp(lam_c)
    kh, rh = kap * e_ex, r * e_in
    bc, kc = bb * e_inv, kt * e_inv
    bt, ktr = bb * e_rem, kt * e_rem
    s_kh = [stack(kh[:, sl]) for sl in sls]
    s_rh = [stack(rh[:, sl]) for sl in sls]
    s_v = [stack(v[:, sl]).astype(BF16) for sl in sls]
    bt_t = [stack(bt[:, sl]).T.astype(BF16) for sl in sls]
    kt_t = [stack(ktr[:, sl]).T.astype(BF16) for sl in sls]

    lhs = [jnp.concatenate([s_kh[p], s_rh[p]], axis=0).astype(BF16) for p in pairs]
    rhs = [jnp.concatenate([stack(bc[:, sl]), stack(kc[:, sl])], axis=0).astype(BF16) for sl in sls]
    gram = [_dot(lhs[p], rhs[p], _NT) for p in pairs]
    l_k = [jnp.where(strict, g[:n2, n2:], 0.0).astype(BF16) for g in gram]
    a_b = [jnp.where(incl, g[n2:, :n2], 0.0).astype(BF16) for g in gram]
    a_k = [jnp.where(incl, g[n2:, n2:], 0.0).astype(BF16) for g in gram]

    pw = [jnp.where(strict, -g[:n2, :n2], 0.0) for g in gram]
    tinv = [eye + x for x in pw]
    for _ in range(5):
        pwb = [x.astype(BF16) for x in pw]
        pw = [_dot(x, x) for x in pwb]
        tinv = [tinv[p] + _dot(tinv[p].astype(BF16), pw[p].astype(BF16)) for p in pairs]
    tinv = [x.astype(BF16) for x in tinv]

    lkv = [_dot(l_k[p], s_v[p]).astype(BF16) for p in pairs]
    wu = [_dot(tinv[p], jnp.concatenate([s_kh[p].astype(BF16), lkv[p]], axis=1)).astype(BF16) for p in pairs]
    mn = [_dot(bt_t[p], wu[p]) for p in pairs]
    ktv = [_dot(kt_t[p], s_v[p]) for p in pairs]
    ab_wu = [_dot(a_b[p], wu[p]) for p in pairs]
    akv = [_dot(a_k[p], s_v[p]) for p in pairs]
    m_mat = [eye * g_c[:, sls[p]] - mn[p][:, :LANES] for p in pairs]
    step_lhs = [jnp.concatenate([s_rh[p] - ab_wu[p][:, :LANES], m_mat[p]], axis=0).astype(BF16) for p in pairs]
    upd = [_dot(step_lhs[p], s_ref[p].astype(BF16)) for p in pairs]
    for p in pairs:
        s_ref[p] = upd[p][n2:] + (ktv[p] - mn[p][:, LANES:])
    for p in pairs:
        s_y = upd[p][:n2] + (akv[p] - ab_wu[p][:, LANES:])
        y = s_y[0:CHUNK, :] + s_y[CHUNK:2 * CHUNK, :]
        sl = sls[p]
        rw_ref[:, sl] = _post_math(y, bonus[:, sl], g[:, sl], lnw_ref[:, sl], lnb_ref[:, sl], ones_bd)

    @pl.when(c == pl.num_programs(1) - 1)
    def _():
        sfin_ref[0] = s_ref[...]


def _chunk_call(p_main, prep_w, lnw, lnb, tri, nb, t):
    nc = t // CHUNK
    tok = lambda n, cb: pl.BlockSpec((CHUNK, n), lambda b, c: (b * nc + c, cb))
    full = lambda a: pl.BlockSpec(a.shape, lambda b, c: (0,) * a.ndim)
    return pl.pallas_call(
        _chunk_kernel,
        out_shape=(jax.ShapeDtypeStruct((nb * t, WIDTH), BF16),
                   jax.ShapeDtypeStruct((nb, PAIRS, LANES, LANES), F32)),
        grid=(nb, nc),
        in_specs=[tok(3 * WIDTH, 0), tok(4 * LANES, C_LORA // (4 * LANES))]
        + [full(a) for a in prep_w] + [full(lnw), full(lnb), full(tri)],
        out_specs=(tok(WIDTH, 0), pl.BlockSpec((1, PAIRS, LANES, LANES), lambda b, c: (b, 0, 0, 0))),
        scratch_shapes=[pltpu.VMEM((PAIRS, LANES, LANES), F32), pltpu.VMEM((1, 3 * WIDTH), F32),
                        pltpu.VMEM((1, 4 * LANES), F32)],
        compiler_params=_params(("arbitrary", "arbitrary"), 48),
        name="rwkv_chunk_scan",
    )(p_main, p_main, *prep_w, lnw, lnb, tri)


def _row_to_cols(row):
    return jnp.broadcast_to(row, (LANES, LANES)).T


def _cols_to_row(cols):
    return jnp.broadcast_to(cols, (LANES, LANES)).T[0:1, :]


def _step_kernel(s_ref, logw_ref, kap_ref, bb_ref, kt_ref, r_ref, v_ref, so_ref, y_ref):
    s = s_ref[0]
    sa = -jnp.sum(s * kap_ref[0], axis=-1, keepdims=True)
    decayed = s * jnp.exp(logw_ref[0]) + sa * bb_ref[0]
    for p in range(PAIRS):
        v_cols = _row_to_cols(v_ref[0, :, p * LANES:(p + 1) * LANES])
        ys = []
        for hh in range(2):
            h = 2 * p + hh
            s_new = decayed[h] + v_cols[hh * HEAD_DIM:(hh + 1) * HEAD_DIM, 0:HEAD_DIM] * kt_ref[0, h]
            so_ref[0, h] = s_new
            ys.append(jnp.sum(s_new * r_ref[0, h], axis=-1, keepdims=True))
        y_ref[0, :, p * LANES:(p + 1) * LANES] = _cols_to_row(jnp.concatenate(ys, axis=0))


def _step_call(state, logw, kap, bb, kt, r, v):
    nb = state.shape[0]
    row = lambda x: x.reshape(nb, HEADS, 1, HEAD_DIM)
    s_spec = pl.BlockSpec((1, HEADS, HEAD_DIM, HEAD_DIM), lambda b: (b, 0, 0, 0))
    r_spec = pl.BlockSpec((1, HEADS, 1, HEAD_DIM), lambda b: (b, 0, 0, 0))
    w_spec = pl.BlockSpec((1, 1, WIDTH), lambda b: (b, 0, 0))
    so, y = pl.pallas_call(
        _step_kernel,
        out_shape=(jax.ShapeDtypeStruct(state.shape, F32), jax.ShapeDtypeStruct((nb, 1, WIDTH), F32)),
        grid=(nb,),
        in_specs=[s_spec] + [r_spec] * 5 + [w_spec],
        out_specs=(s_spec, w_spec),
        compiler_params=_params(("arbitrary",), 32),
        name="rwkv_step",
    )(state, row(logw), row(kap), row(bb), row(kt), row(r), v.reshape(nb, 1, WIDTH))
    return so, y.reshape(nb, WIDTH)


def _sortable(x):
    x = jnp.where(x == 0.0, 0.0, x)
    bits = pltpu.bitcast(x, I32)
    return jnp.where(bits < 0, bits ^ jnp.int32(0x7FFFFFFF), bits)


def _select_topk(keys_ref, score_of, valid_of, nkc, rows, kc_size, col_of, idx_bits):
    def lane_tiles(x, op):
        out = x[:, 0:LANES]
        for q in range(1, kc_size // LANES):
            out = op(out, x[:, q * LANES:(q + 1) * LANES])
        return out

    def count(pred):
        def body(kc, acc):
            return acc + lane_tiles(jnp.where(pred(kc), 1.0, 0.0), jnp.add)
        acc = lax.fori_loop(0, nkc, body, jnp.zeros((rows, LANES), F32))
        return jnp.sum(acc, axis=-1, keepdims=True)

    kf = float(TOPK)
    thr = jnp.where(count(lambda kc: keys_ref[kc] >= 0) >= kf, jnp.int32(0), jnp.int32(INT_MIN))

    def bit_body(it, thr):
        cand = thr + lax.shift_left(jnp.int32(1), jnp.int32(30) - it)
        return jnp.where(count(lambda kc: keys_ref[kc] >= cand) >= kf, cand, thr)

    thr = lax.fori_loop(0, 31, bit_body, thr)
    need = kf - count(lambda kc: keys_ref[kc] > thr)
    split = (count(lambda kc: keys_ref[kc] == thr) > need) & (thr > jnp.int32(INT_MIN))
    any_split = jnp.max(jnp.where(split, 1.0, 0.0)) > 0.0

    def tie_break():
        def tie_body(it, x):
            cand = x + lax.shift_left(jnp.int32(1), jnp.int32(idx_bits - 1) - it)
            c = count(lambda kc: (keys_ref[kc] == thr) & (col_of(kc) <= cand))
            return jnp.where(c < need, cand, x)

        return lax.fori_loop(0, idx_bits, tie_body, jnp.full((rows, 1), -1, I32)) + 1

    jb = lax.cond(any_split, tie_break, lambda: jnp.full((rows, 1), INT_MAX, I32))

    def picked(kc):
        key = keys_ref[kc]
        return ((key > thr) | ((key == thr) & (col_of(kc) <= jb))) & valid_of(kc)

    def edge_body(kc, carry):
        lo, hi = carry
        p, sc = picked(kc), score_of(kc)
        lo = jnp.minimum(lo, lane_tiles(jnp.where(p, sc, jnp.inf), jnp.minimum))
        hi = jnp.maximum(hi, lane_tiles(jnp.where(valid_of(kc) & ~p, sc, -jnp.inf), jnp.maximum))
        return lo, hi

    lo, hi = lax.fori_loop(0, nkc, edge_body, (jnp.full((rows, LANES), jnp.inf, F32),
                                               jnp.full((rows, LANES), -jnp.inf, F32)))
    t_sel = jnp.where(split, jnp.inf, jnp.min(lo, axis=-1, keepdims=True))
    t_uns = jnp.max(hi, axis=-1, keepdims=True)
    thr_bits = jnp.where(split, thr, jnp.int32(INT_MAX))

    def chosen(kc):
        sc, key = score_of(kc), keys_ref[kc]
        nearer = (sc - t_uns) > (t_sel - sc)
        by_bits = (key > thr_bits) | ((key == thr_bits) & (col_of(kc) <= jb))
        return (nearer | by_bits) & valid_of(kc)

    return chosen


TQ = 128
KC = 512


M_INIT = -1e30


def _idx_key_pack(kidx):
    hi, lo = _split(kidx)
    return jnp.concatenate([hi, lo, hi, jnp.zeros_like(hi)], axis=1)


def _attn_kernel(qidx_ref, widx_ref, kcat_ref, aq_ref, kt_ref, vt_ref,
                 att_ref, keys_ref, bias_ref, qcat_ref, d_ref, qm_ref, m_ref, l_ref, acc_ref):
    i = pl.program_id(1)
    nkc = lax.shift_right_logical((i + 1) * TQ + (KC - 1), int(np.log2(KC)))
    row = i * TQ + lax.broadcasted_iota(I32, (TQ, KC), 0)
    lane_kc = lax.broadcasted_iota(I32, (TQ, KC), 1)
    col_of = lambda kc: kc * KC + lane_kc
    head_rows = lambda h: slice(h * TQ, (h + 1) * TQ)
    first = lax.broadcasted_iota(I32, (1, LANES), 1) < HEAD_DIM

    for p in range(PAIRS):
        x = qidx_ref[:, p * LANES:(p + 1) * LANES]
        hi = x.astype(BF16).astype(F32)
        lo = x - hi
        hi_sw = pltpu.roll(hi, HEAD_DIM, 1)
        lo_sw = pltpu.roll(lo, HEAD_DIM, 1)
        for hh in range(2):
            rows = head_rows(2 * p + hh)
            own_hi, other_hi = (hi, hi_sw) if hh == 0 else (hi_sw, hi)
            qcat_ref[rows, 0:LANES] = jnp.where(first, own_hi, other_hi).astype(BF16)
            qcat_ref[rows, LANES:2 * LANES] = jnp.where(first, lo if hh == 0 else lo_sw, 0.0).astype(BF16)
    wv = widx_ref[:, IDX_DIM:IDX_DIM + HEADS] * (HEADS ** -0.5 * IDX_DIM ** -0.5)

    def score_body(kc, carry):
        d_ref[...] = _dot(qcat_ref[...], kcat_ref[kc])
        acc = jnp.zeros((TQ, KC), F32)
        for h in range(HEADS):
            acc = acc + jnp.maximum(d_ref[head_rows(h), :], 0.0) * wv[:, h:h + 1]
        keys_ref[kc] = jnp.where(col_of(kc) <= row, _sortable(acc), jnp.int32(INT_MIN))
        bias_ref[kc] = acc
        return carry

    lax.fori_loop(0, nkc, score_body, 0)

    chosen = _select_topk(keys_ref, lambda kc: bias_ref[kc], lambda kc: col_of(kc) <= row, nkc, TQ, KC, col_of, 12)

    def bias_body(kc, carry):
        bias_ref[kc] = jnp.where(chosen(kc), 0.0, 2.0 * M_INIT)
        return carry

    lax.fori_loop(0, nkc, bias_body, 0)

    for p in range(PAIRS):
        aq = aq_ref[:, p * LANES:(p + 1) * LANES] * (HEAD_DIM ** -0.5)
        qm_ref[p, 0:TQ, :] = jnp.where(first, aq, 0.0).astype(BF16)
        qm_ref[p, TQ:2 * TQ, :] = jnp.where(first, 0.0, aq).astype(BF16)
    def lane_tiles(x, op):
        out = x[:, 0:LANES]
        for q in range(1, KC // LANES):
            out = op(out, x[:, q * LANES:(q + 1) * LANES])
        return out

    halves = (slice(0, TQ), slice(TQ, 2 * TQ))

    m_ref[...] = jnp.full(m_ref.shape, M_INIT, F32)
    l_ref[...] = jnp.zeros(l_ref.shape, F32)
    acc_ref[...] = jnp.zeros(acc_ref.shape, F32)

    def pv_body(kc, carry):
        bias = bias_ref[kc]
        sls = [slice(p * LANES, (p + 1) * LANES) for p in range(PAIRS)]
        s = [_dot(qm_ref[p], kt_ref[kc, sls[p], :]) for p in range(PAIRS)]
        es = []
        for p in range(PAIRS):
            e = []
            for rs in halves:
                sh = s[p][rs] + bias
                m_old = m_ref[p, rs, :]
                m_new = jnp.maximum(m_old, jnp.max(lane_tiles(sh, jnp.maximum), axis=-1, keepdims=True))
                alpha = jnp.exp(m_old - m_new)
                x = jnp.exp(sh - m_new)
                m_ref[p, rs, :] = m_new
                l_ref[p, rs, :] = alpha * l_ref[p, rs, :] + lane_tiles(x, jnp.add)
                acc_ref[p, rs, :] = alpha * acc_ref[p, rs, :]
                e.append(x.astype(BF16))
            es.append(jnp.concatenate(e, axis=0))
        pv = [_dot(es[p], vt_ref[kc, sls[p], :], _NT) for p in range(PAIRS)]
        for p in range(PAIRS):
            acc_ref[p] += pv[p]
        return carry

    lax.fori_loop(0, nkc, pv_body, 0)

    for p in range(PAIRS):
        o = acc_ref[p] / jnp.sum(l_ref[p], axis=-1, keepdims=True)
        att_ref[:, p * LANES:(p + 1) * LANES] = jnp.where(first, o[0:TQ], o[TQ:2 * TQ]).astype(BF16)


def _attn_call(p_idx, kcat_t, p_main, kt, vt, nb, t):
    nq = t // TQ
    nkc = t // KC
    qtok = lambda n, cb: pl.BlockSpec((TQ, n), lambda b, i: (b * nq + i, cb))
    seq3 = lambda n: pl.BlockSpec((nkc, n, KC), lambda b, i: (b, 0, 0), pipeline_mode=pl.Buffered(1))
    return pl.pallas_call(
        _attn_kernel,
        out_shape=jax.ShapeDtypeStruct((nb * t, WIDTH), BF16),
        grid=(nb, nq),
        in_specs=[qtok(WIDTH, 0), qtok(LANES, WIDTH // LANES), seq3(4 * IDX_DIM),
                  qtok(WIDTH, C_AQ // WIDTH), seq3(WIDTH), seq3(WIDTH)],
        out_specs=qtok(WIDTH, 0),
        scratch_shapes=[pltpu.VMEM((nkc, TQ, KC), I32), pltpu.VMEM((nkc, TQ, KC), F32),
                        pltpu.VMEM((HEADS * TQ, 2 * LANES), BF16), pltpu.VMEM((HEADS * TQ, KC), F32),
                        pltpu.VMEM((PAIRS, 2 * TQ, LANES), BF16), pltpu.VMEM((PAIRS, 2 * TQ, 1), F32),
                        pltpu.VMEM((PAIRS, 2 * TQ, LANES), F32), pltpu.VMEM((PAIRS, 2 * TQ, LANES), F32)],
        compiler_params=_params(("arbitrary", "arbitrary"), 48),
        name="dsa_prompt_attention",
    )(p_idx, p_idx, kcat_t, p_main, kt, vt)


PG = 8
SROWS = 72
SCOLS = SROWS * PAGE


PG_IDX = 16


def _sidx_kernel(pt_ref, q_ref, w_ref, knew_ref, *rest, npages):
    pages, o_ref = rest[:PG_IDX], rest[PG_IDX]
    j = pl.program_id(1)
    qh, ql = _split(q_ref[0])
    w = w_ref[0] * (HEADS ** -0.5 * IDX_DIM ** -0.5)

    def score(keys):
        kh, kl = _split(keys)
        d = _dot(qh, kh) + (_dot(qh, kl) + _dot(ql, kh))
        return jnp.sum(jnp.maximum(d, 0.0) * w, axis=0, keepdims=True)

    @pl.when(j == 0)
    def _():
        o_ref[0] = jnp.zeros((SROWS, PAGE), F32)
        o_ref[0, npages:npages + 1, :] = score(knew_ref[0])

    sc = score(jnp.concatenate([pages[g][0] for g in range(PG_IDX)], axis=1))
    for g in range(PG_IDX):
        o_ref[0, pl.ds(j * PG_IDX + g, 1), :] = sc[:, g * PAGE:(g + 1) * PAGE]


def _sidx_call(page_table, q, w, knew, cache_kidx):
    nb, npages = page_table.shape
    one = lambda s: pl.BlockSpec((1,) + s, lambda b, j, pt: (b, 0, 0))
    page = lambda g: pl.BlockSpec((1, IDX_DIM, PAGE), lambda b, j, pt: (pt[b, j * PG_IDX + g], 0, 0))
    return pl.pallas_call(
        functools.partial(_sidx_kernel, npages=npages),
        out_shape=jax.ShapeDtypeStruct((nb, SROWS, PAGE), F32),
        grid_spec=pltpu.PrefetchScalarGridSpec(
            num_scalar_prefetch=1,
            grid=(nb, npages // PG_IDX),
            in_specs=[one((HEADS, IDX_DIM)), one((HEADS, 1)), one((IDX_DIM, PAGE))]
            + [page(g) for g in range(PG_IDX)],
            out_specs=one((SROWS, PAGE)),
        ),
        compiler_params=_params(("arbitrary", "arbitrary"), 32),
        name="dsa_sample_index_scores",
    )(page_table, q, w, knew, *([cache_kidx] * PG_IDX))


SKC = 512
SNKC = SCOLS // SKC


def _ssel_kernel(sc_ref, bias_ref, keys_ref, chunks_ref, *, n_valid):
    rows = sc_ref.shape[0]
    lane_kc = lax.broadcasted_iota(I32, (rows, SKC), 1)
    col_of = lambda kc: kc * SKC + lane_kc
    valid_of = lambda kc: col_of(kc) < n_valid
    for c in range(SNKC):
        x = sc_ref[:, c * SKC:(c + 1) * SKC]
        chunks_ref[c] = x
        keys_ref[c] = jnp.where(valid_of(c), _sortable(x), jnp.int32(INT_MIN))
    chosen = _select_topk(keys_ref, lambda kc: chunks_ref[kc], valid_of, SNKC, rows, SKC, col_of, 14)
    for c in range(SNKC):
        bias_ref[:, c * SKC:(c + 1) * SKC] = jnp.where(chosen(c), 0.0, NEG)


def _ssel_call(scores, n_valid):
    rows = scores.shape[0]
    return pl.pallas_call(
        functools.partial(_ssel_kernel, n_valid=n_valid),
        out_shape=jax.ShapeDtypeStruct(scores.shape, F32),
        scratch_shapes=[pltpu.VMEM((SNKC, rows, SKC), I32), pltpu.VMEM((SNKC, rows, SKC), F32)],
        name="dsa_sample_select",
    )(scores)


def _sattn_kernel(pt_ref, q_ref, kn_ref, vn_ref, bias_ref, *rest, npages):
    kpages, vpages = rest[:PG], rest[PG:2 * PG]
    o_ref, qb_ref, m_ref, l_ref, acc_ref = rest[2 * PG:]
    j = pl.program_id(1)
    scale = HEAD_DIM ** -0.5

    @pl.when(j == 0)
    def _():
        on = bias_ref[0, npages:npages + 1, 0:1] == 0.0
        first = on & (lax.broadcasted_iota(I32, (1, PAGE), 1) == 0)
        for p in range(PAIRS):
            sl = slice(p * LANES, (p + 1) * LANES)
            q_cols = _row_to_cols(q_ref[0, :, sl] * scale)
            k_cols = _row_to_cols(kn_ref[0, :, sl])
            v_cols = _row_to_cols(vn_ref[0, :, sl])
            for hh in range(2):
                h = 2 * p + hh
                rows = slice(hh * HEAD_DIM, (hh + 1) * HEAD_DIM)
                qb_ref[h] = q_cols[rows]
                s_new = jnp.sum(q_cols[rows] * k_cols[rows], axis=0, keepdims=True)
                m_ref[h] = jnp.where(on, s_new, NEG)
                l_ref[h] = jnp.broadcast_to(jnp.where(on, 1.0, 0.0), (1, PAGE))
                acc_ref[h] = jnp.where(first, v_cols[rows], 0.0)

    for g in range(PG):
        on = bias_ref[0, pl.ds(j * PG + g, 1), :] == 0.0
        for h in range(HEADS):
            s = jnp.sum(qb_ref[h] * kpages[g][0, h], axis=0, keepdims=True)
            m = m_ref[h]
            m_new = jnp.maximum(m, jnp.max(jnp.where(on, s, NEG), axis=-1, keepdims=True))
            e = jnp.where(on, jnp.exp(s - m_new), 0.0)
            alpha = jnp.exp(m - m_new)
            m_ref[h] = m_new
            l_ref[h] = alpha * l_ref[h] + jnp.sum(e, axis=-1, keepdims=True)
            acc_ref[h] = alpha * acc_ref[h] + e * vpages[g][0, h]

    @pl.when(j == pl.num_programs(1) - 1)
    def _():
        for p in range(PAIRS):
            outs = [jnp.sum(acc_ref[h], axis=-1, keepdims=True) / l_ref[h][:, 0:1] for h in (2 * p, 2 * p + 1)]
            o_ref[0, :, p * LANES:(p + 1) * LANES] = _cols_to_row(jnp.concatenate(outs, axis=0))


def _sattn_call(page_table, q, kn, vn, bias, ck, cv):
    nb, npages = page_table.shape
    one = lambda s: pl.BlockSpec((1,) + s, lambda b, j, pt: (b,) + (0,) * len(s))
    col = one((1, WIDTH))
    page = lambda g: pl.BlockSpec((1, HEADS, HEAD_DIM, PAGE), lambda b, j, pt: (pt[b, j * PG + g], 0, 0, 0))
    return pl.pallas_call(
        functools.partial(_sattn_kernel, npages=npages),
        out_shape=jax.ShapeDtypeStruct((nb, 1, WIDTH), F32),
        grid_spec=pltpu.PrefetchScalarGridSpec(
            num_scalar_prefetch=1,
            grid=(nb, npages // PG),
            in_specs=[col, col, col, one((SROWS, PAGE))] + [page(g) for g in range(PG)] * 2,
            out_specs=col,
            scratch_shapes=[pltpu.VMEM((HEADS, HEAD_DIM, PAGE), F32), pltpu.VMEM((HEADS, 1, PAGE), F32),
                            pltpu.VMEM((HEADS, 1, PAGE), F32), pltpu.VMEM((HEADS, HEAD_DIM, PAGE), F32)],
        ),
        compiler_params=_params(("arbitrary", "arbitrary"), 40),
        name="dsa_sample_attention",
    )(page_table, q, kn, vn, bias, *([ck] * PG), *([cv] * PG))


def _merge_kernel(rw_ref, att_ref, wr_ref, wa_ref, ga_ref, gb_ref, o_ref):
    yr = _dot(rw_ref[...], wr_ref[...])
    ya = _dot(att_ref[...], wa_ref[...])
    o_ref[...] = (_sigmoid(ga_ref[...]) * yr + _sigmoid(gb_ref[...]) * ya).astype(BF16)


def _merge_call(rw, att, w_r, w_a, p_main, tm, tn):
    m = rw.shape[0]
    lhs = pl.BlockSpec((tm, WIDTH), lambda j, i: (i, 0))
    rhs = pl.BlockSpec((WIDTH, tn), lambda j, i: (0, j))
    gate = lambda c0: pl.BlockSpec((tm, tn), lambda j, i: (i, c0 // tn + j))
    return pl.pallas_call(
        _merge_kernel,
        out_shape=jax.ShapeDtypeStruct((m, D_MODEL), BF16),
        grid=(D_MODEL // tn, m // tm),
        in_specs=[lhs, lhs, rhs, rhs, gate(C_GA), gate(C_GB)],
        out_specs=pl.BlockSpec((tm, tn), lambda j, i: (i, j)),
        compiler_params=_params(("arbitrary", "arbitrary"), 32),
        name="mixer_merge",
    )(rw, att, w_r, w_a, p_main, p_main)


def _rms(x, g):
    return x * lax.rsqrt(jnp.mean(x * x, axis=-1, keepdims=True) + RMS_EPS) * g


def _outproj_kernel(m_ref, w_ref, x_ref, gpost_ref, gpre_ref, h_ref, f_ref):
    h = x_ref[...] + _rms(_dot(m_ref[...], w_ref[...]), gpost_ref[...])
    h_ref[...] = h
    f_ref[...] = _rms(h, gpre_ref[...]).astype(BF16)


def _outproj_call(mix, w_out, x, g_post, g_pre, tm):
    m = x.shape[0]
    tok = pl.BlockSpec((tm, D_MODEL), lambda i: (i, 0))
    full = lambda a: pl.BlockSpec(a.shape, lambda i: (0, 0))
    return pl.pallas_call(
        _outproj_kernel,
        out_shape=(jax.ShapeDtypeStruct((m, D_MODEL), F32), jax.ShapeDtypeStruct((m, D_MODEL), BF16)),
        grid=(m // tm,),
        in_specs=[tok, full(w_out), tok, full(g_post), full(g_pre)],
        out_specs=(tok, tok),
        compiler_params=_params(("arbitrary",), 40),
        name="mixer_out_proj",
    )(mix, w_out, x, g_post, g_pre)


def _ffn_kernel(f_ref, wg_ref, wu_ref, wd_ref, h_ref, gp_ref, y_ref, acc_ref):
    j = pl.program_id(1)

    @pl.when(j == 0)
    def _():
        acc_ref[...] = jnp.zeros(acc_ref.shape, F32)

    f = f_ref[...]
    gate = _dot(f, wg_ref[...])
    t = (gate * _sigmoid(gate) * _dot(f, wu_ref[...])).astype(BF16)
    acc_ref[...] += _dot(t, wd_ref[...])

    @pl.when(j == pl.num_programs(1) - 1)
    def _():
        y_ref[...] = h_ref[...] + _rms(acc_ref[...], gp_ref[...])


def _ffn_call(f, w_gate, w_up, w_down, h, g_post, tm, tf):
    m = f.shape[0]
    d_ff = w_gate.shape[1]
    tok = pl.BlockSpec((tm, D_MODEL), lambda i, j: (i, 0))
    return pl.pallas_call(
        _ffn_kernel,
        out_shape=jax.ShapeDtypeStruct((m, D_MODEL), F32),
        grid=(m // tm, d_ff // tf),
        in_specs=[tok, pl.BlockSpec((D_MODEL, tf), lambda i, j: (0, j)), pl.BlockSpec((D_MODEL, tf), lambda i, j: (0, j)),
                  pl.BlockSpec((tf, D_MODEL), lambda i, j: (j, 0)), tok, pl.BlockSpec((1, D_MODEL), lambda i, j: (0, 0))],
        out_specs=tok,
        scratch_shapes=[pltpu.VMEM((tm, D_MODEL), F32)],
        compiler_params=_params(("arbitrary", "arbitrary"), 48),
        name="swiglu_ffn",
    )(f, w_gate, w_up, w_down, h, g_post)


def _pad_cols(a, n):
    return jnp.pad(a, ((0, 0), (0, n - a.shape[1])))


def _regroup_cols(a):
    w3, ls = 3 * WIDTH, LORA_SMALL
    lora = jnp.concatenate([_pad_cols(a[:, w3:w3 + ls], LANES), _pad_cols(a[:, w3 + ls:w3 + 2 * ls], LANES),
                            a[:, w3 + 2 * ls:]], axis=1)
    return a[:, :w3], lora


def _ungroup_cols(rkv, lora):
    ls = LORA_SMALL
    return jnp.concatenate([rkv, lora[:, :ls], lora[:, LANES:LANES + ls], lora[:, 2 * LANES:]], axis=1)


def kernel(x_prompt, x_sample, cache_k, cache_v, cache_kidx, state_wkv, state_shift, page_table, g_pre_mix, w_in,
           rwkv_mu, w0, w_w2, a0, w_a2, w_g2, k_k, k_a, r_k, ln_x_w, ln_x_b, w_o_rwkv, w_o_attn, w_out,
           g_post_mix, g_pre_ffn, w_gate, w_up, w_down, g_post_ffn):
    nb, t, _ = x_prompt.shape
    ns = x_sample.shape[0]
    row = lambda a: a.reshape(1, -1).astype(F32)

    o_q = RWKV_COLS
    o_qi = o_q + 3 * WIDTH
    o_ki = o_qi + WIDTH
    o_ga = o_ki + IDX_DIM + HEADS
    w_rkv, w_lora = _regroup_cols(w_in[:, :RWKV_COLS])
    o_k, o_v = o_q + WIDTH, o_q + 2 * WIDTH
    w_main = jnp.concatenate([w_rkv, w_in[:, o_q:o_k], w_in[:, o_ga:], w_lora], axis=1).astype(BF16)
    w_kv = w_in[:, o_k:o_qi].astype(BF16)
    w_kt, w_vt = w_kv[:, :WIDTH].T, w_kv[:, WIDTH:].T
    w_idx_hi, w_idx_lo = _split(jnp.concatenate([w_in[:, o_qi:o_ki], _pad_cols(w_in[:, o_ki:o_ga], LANES)], axis=1))
    mu_rkv, mu_lora = _regroup_cols(row(rwkv_mu))
    pad_rows = lambda a: jnp.pad(a, ((0, LANES - a.shape[0]), (0, 0))).astype(BF16)
    lane = np.arange(LANES)
    ones_bd = jnp.asarray((lane[:, None] // HEAD_DIM) == (lane[None, :] // HEAD_DIM), BF16)
    tri = jnp.asarray(np.arange(CHUNK)[:, None] >= np.arange(CHUNK)[None, :], BF16)
    prep_w = [mu_rkv, mu_lora, row(w0), row(a0), row(k_k), row(k_a), row(r_k), pad_rows(w_w2), pad_rows(w_a2),
              w_g2.astype(BF16), ones_bd]
    lnw, lnb = row(ln_x_w), row(ln_x_b)
    w_or, w_oa, w_o = w_o_rwkv.astype(BF16), w_o_attn.astype(BF16), w_out.astype(BF16)
    w_g, w_u, w_d = w_gate.astype(BF16), w_up.astype(BF16), w_down.astype(BF16)
    gains = [row(g_pre_mix), row(g_post_mix), row(g_pre_ffn), row(g_post_ffn)]

    def project(x2d, tm, transposed):
        u = _rms_call(x2d, gains[0], min(tm, 512), transposed)
        p_main = _mm_call([u[0]], [w_main], [(0, 0)], F32, min(tm, 512), N_MAIN // 4, "in_proj")
        p_idx = _mm_call([u[0], u[1]], [w_idx_hi, w_idx_lo], [(0, 0), (0, 1), (1, 0)], F32, tm, 384, "in_proj_idx")
        return p_main, p_idx, u

    def finish(x2d, p_main, rw, att, tm, tn, tm_ffn):
        mix = _merge_call(rw, att, w_or, w_oa, p_main, tm, tn)
        h, f = _outproj_call(mix, w_o, x2d, gains[1], gains[2], min(tm, 256))
        return _ffn_call(f, w_g, w_u, w_d, h, gains[3], tm_ffn, 512)

    m = nb * t
    xp = x_prompt.reshape(m, D_MODEL)
    p_main, p_idx, u_p = project(xp, 1024, True)
    rw, s_fin = _chunk_call(p_main, prep_w, lnw, lnb, tri, nb, t)
    kidx = p_idx[:, WIDTH:WIDTH + IDX_DIM]
    k_t, k_tc = _proj_t_call(w_kt, u_p[2], nb, t, KC)
    v_t, v_tc = _proj_t_call(w_vt, u_p[2], nb, t, KC)
    chunk_t = lambda a: jnp.swapaxes(a.reshape(m // KC, KC, a.shape[1]), 1, 2)
    att = _attn_call(p_idx, chunk_t(_idx_key_pack(kidx)), p_main, k_tc, v_tc, nb, t)
    tok_major = lambda a: jnp.transpose(a.reshape(nb, HEADS, HEAD_DIM, t), (0, 3, 1, 2))
    k_p, v_p = tok_major(k_t), tok_major(v_t)
    y_p = finish(xp, p_main, rw, att, 1024, 512, 512)
    blocks = jnp.stack([s_fin[:, :, :HEAD_DIM, :HEAD_DIM], s_fin[:, :, HEAD_DIM:, HEAD_DIM:]], axis=2)
    wkv_p = jnp.swapaxes(blocks.reshape(nb, HEADS, HEAD_DIM, HEAD_DIM), -1, -2)
    last = p_main.reshape(nb, t, N_MAIN)[:, -1]
    shift_p = _ungroup_cols(last[:, :3 * WIDTH], last[:, C_LORA:])

    xs = x_sample.reshape(ns, D_MODEL)
    ps_main, ps_idx, u_s = project(xs, ns, False)
    ps_kv = _mm_call([u_s[0]], [w_kv], [(0, 0)], F32, ns, 512, "in_proj_kv")
    prev_rkv, prev_lora = _regroup_cols(state_shift)
    sf = _prep_call(ps_main, prev_rkv, prev_lora, lax.optimization_barrier(prep_w))
    logw_s, kap_s, bb_s, kt_s, r_s, v_s, bonus_s, g_s = sf
    wkv_s, y_s = _step_call(state_wkv, logw_s, kap_s, bb_s, kt_s, r_s, v_s)
    rw_s = _post_call(y_s, bonus_s, g_s, lnw, lnb, ones_bd)

    npages = page_table.shape[1]
    past = npages * PAGE
    q_i = ps_idx[:, :WIDTH].reshape(ns, HEADS, IDX_DIM)
    w_i = ps_idx[:, WIDTH + IDX_DIM:WIDTH + IDX_DIM + HEADS].reshape(ns, HEADS, 1)
    kidx_s = ps_idx[:, WIDTH:WIDTH + IDX_DIM]
    knew = jnp.pad(kidx_s[:, :, None], ((0, 0), (0, 0), (0, PAGE - 1)))
    scores = _sidx_call(page_table, q_i, w_i, knew, jnp.transpose(cache_kidx, (0, 2, 1)))
    bias = _ssel_call(scores.reshape(ns, SCOLS), past + 1).reshape(ns, SROWS, PAGE)
    k_s, v_s_att = ps_kv[:, :WIDTH], ps_kv[:, WIDTH:]
    seq = lambda a: a.reshape(ns, 1, WIDTH)
    att_s = _sattn_call(page_table, seq(ps_main[:, C_AQ:C_AQ + WIDTH]), seq(k_s), seq(v_s_att), bias,
                        jnp.transpose(cache_k, (0, 2, 3, 1)), jnp.transpose(cache_v, (0, 2, 3, 1)))
    y_smp = finish(xs, ps_main, rw_s, att_s.reshape(ns, WIDTH).astype(BF16), ns, 512, ns)
    shift_s = _ungroup_cols(ps_main[:, :3 * WIDTH], ps_main[:, C_LORA:])

    hd = (HEADS, HEAD_DIM)
    return (y_p.reshape(nb, t, D_MODEL), y_smp.reshape(ns, 1, D_MODEL),
            k_p, v_p, kidx.reshape(nb, t, IDX_DIM),
            wkv_p, shift_p,
            k_s.reshape(ns, 1, *hd), v_s_att.reshape(ns, 1, *hd), kidx_s.reshape(ns, 1, IDX_DIM),
            wkv_s, shift_s)
```

```python
import functools

import numpy as np
import jax
import jax.numpy as jnp
from jax import lax
from jax.experimental import pallas as pl
from jax.experimental.pallas import tpu as pltpu

F32, BF16, I32 = jnp.float32, jnp.bfloat16, jnp.int32

D_MODEL = 2048
HEADS = 16
HEAD_DIM = 64
WIDTH = HEADS * HEAD_DIM
LORA_SMALL = 96
LORA_GATE = 256
RWKV_COLS = 3 * WIDTH + 2 * LORA_SMALL + LORA_GATE
IDX_DIM = 64
TOPK = 256
PAGE = 128
RMS_EPS = 1e-6
GN_EPS = 64e-5

LANES = 128
PAIRS = WIDTH // LANES

C_AQ, C_GA, C_GB, C_LORA = 3 * WIDTH, 4 * WIDTH, 4 * WIDTH + D_MODEL, 4 * WIDTH + 2 * D_MODEL
N_MAIN = C_LORA + 4 * LANES

INT_MIN = int(np.iinfo(np.int32).min)
INT_MAX = int(np.iinfo(np.int32).max)
NEG = -1e30

_NN = (((1,), (0,)), ((), ()))
_NT = (((1,), (1,)), ((), ()))


def _params(sem, vmem_mb):
    return pltpu.CompilerParams(dimension_semantics=sem, vmem_limit_bytes=vmem_mb * 1024 * 1024)


def _dot(a, b, dims=_NN):
    return lax.dot_general(a, b, dims, preferred_element_type=F32)


def _split(x):
    hi = x.astype(BF16)
    return hi, (x - hi.astype(F32)).astype(BF16)


def _sigmoid(x):
    return 1.0 / (1.0 + jnp.exp(-x))


def _headsum(x, ones_bd):
    hi, lo = _split(x)
    return _dot(hi, ones_bd) + _dot(lo, ones_bd)


def _headsum_full(x, ones_bd):
    return jnp.concatenate(
        [_headsum(x[:, p * LANES:(p + 1) * LANES], ones_bd) for p in range(PAIRS)], axis=1)


def _rms_kernel(x_ref, g_ref, hi_ref, lo_ref, *t_ref):
    x = x_ref[...]
    y = x * lax.rsqrt(jnp.mean(x * x, axis=-1, keepdims=True) + RMS_EPS) * g_ref[...]
    hi = y.astype(BF16)
    hi_ref[...] = hi
    lo_ref[...] = (y - hi.astype(F32)).astype(BF16)
    if t_ref:
        t_ref[0][...] = hi.astype(F32).T.astype(BF16)


def _rms_call(x, g, tm, transposed=False):
    m, d = x.shape
    spec = pl.BlockSpec((tm, d), lambda i: (i, 0))
    shapes = [jax.ShapeDtypeStruct((m, d), BF16), jax.ShapeDtypeStruct((m, d), BF16)]
    specs = [spec, spec]
    if transposed:
        shapes.append(jax.ShapeDtypeStruct((d, m), BF16))
        specs.append(pl.BlockSpec((d, tm), lambda i: (0, i)))
    return pl.pallas_call(
        _rms_kernel,
        out_shape=tuple(shapes),
        grid=(m // tm,),
        in_specs=[spec, pl.BlockSpec((1, d), lambda i: (0, 0))],
        out_specs=tuple(specs),
        compiler_params=_params(("arbitrary",), 40),
        name="rms_norm",
    )(x, g)


def _proj_t_kernel(w_ref, ut_ref, o32_ref, o16_ref):
    y = _dot(w_ref[...], ut_ref[...])
    o32_ref[0] = y
    o16_ref[0] = y.astype(BF16)


def _proj_t_call(w_t, u_t, nb, t, tn):
    n, d = w_t.shape
    nc = t // tn
    return pl.pallas_call(
        _proj_t_kernel,
        out_shape=(jax.ShapeDtypeStruct((nb, n, t), F32), jax.ShapeDtypeStruct((nb * nc, n, tn), BF16)),
        grid=(nb, nc),
        in_specs=[pl.BlockSpec((n, d), lambda b, j: (0, 0)), pl.BlockSpec((d, tn), lambda b, j: (0, b * nc + j))],
        out_specs=(pl.BlockSpec((1, n, tn), lambda b, j: (b, 0, j)),
                   pl.BlockSpec((1, n, tn), lambda b, j: (b * nc + j, 0, 0))),
        compiler_params=_params(("arbitrary", "arbitrary"), 40),
        name="in_proj_t",
    )(w_t, u_t)


def _mm_kernel(*refs, na, nb, terms):
    a, b, o = refs[:na], refs[na:na + nb], refs[na + nb]
    acc = None
    for ia, ib in terms:
        d = _dot(a[ia][...], b[ib][...])
        acc = d if acc is None else acc + d
    o[...] = acc.astype(o.dtype)


def _mm_call(a_list, b_list, terms, out_dtype, tm, tn, name):
    m, k = a_list[0].shape
    n = b_list[0].shape[1]
    a_spec = pl.BlockSpec((tm, k), lambda j, i: (i, 0))
    b_spec = pl.BlockSpec((k, tn), lambda j, i: (0, j))
    return pl.pallas_call(
        functools.partial(_mm_kernel, na=len(a_list), nb=len(b_list), terms=terms),
        out_shape=jax.ShapeDtypeStruct((m, n), out_dtype),
        grid=(n // tn, m // tm),
        in_specs=[a_spec] * len(a_list) + [b_spec] * len(b_list),
        out_specs=pl.BlockSpec((tm, tn), lambda j, i: (i, j)),
        compiler_params=_params(("arbitrary", "arbitrary"), 48),
        name=name,
    )(*a_list, *b_list)


def _shift_rows(z, first_row):
    rows = lax.broadcasted_iota(I32, z.shape, 0)
    return jnp.where(rows == 0, first_row, pltpu.roll(z, 1, 0))


N_PREP_W = 11


def _prep_math(z, zl, zp, zlp, w):
    mu_rkv, mu_lora, w0, a0, k_k, k_a, r_k, w_w2, w_a2, w_g2, ones = [x[...] for x in w]
    zs = z + (zp - z) * mu_rkv
    zls = zl + (zlp - zl) * mu_lora
    r, k, v = zs[:, 0:WIDTH], zs[:, WIDTH:2 * WIDTH], zs[:, 2 * WIDTH:3 * WIDTH]
    wd, ad, gd = zls[:, 0:LANES], zls[:, LANES:2 * LANES], zls[:, 2 * LANES:4 * LANES]
    nlw = -(w0 + _dot(jnp.tanh(wd).astype(BF16), w_w2))
    softplus = jnp.maximum(nlw, 0.0) + jnp.log(1.0 + jnp.exp(-jnp.abs(nlw)))
    logw = -jnp.exp(-softplus - 0.5)
    a = _sigmoid(a0 + _dot(ad.astype(BF16), w_a2))
    g = _dot(_sigmoid(gd).astype(BF16), w_g2)
    kkv = k * k_k
    kap = kkv / jnp.maximum(jnp.sqrt(_headsum_full(kkv * kkv, ones)), 1e-12)
    kt = k * (1.0 + (a - 1.0) * k_a)
    bonus = _headsum_full(r * kt * r_k, ones) * v
    return logw, kap, kap * a, kt, r, v, bonus, g


def _prep_kernel(rkv_ref, lora_ref, p_rkv_ref, p_lora_ref, *rest):
    w, outs = rest[:N_PREP_W], rest[N_PREP_W:]
    vals = _prep_math(rkv_ref[...], lora_ref[...], p_rkv_ref[...], p_lora_ref[...], w)
    for o, x in zip(outs, vals):
        o[...] = x


def _prep_call(p_main, prev_rkv, prev_lora, wts):
    n = p_main.shape[0]
    tok = lambda c, cb: pl.BlockSpec((n, c), lambda i: (0, cb))
    full = lambda a: pl.BlockSpec(a.shape, lambda i: (0,) * a.ndim)
    return pl.pallas_call(
        _prep_kernel,
        out_shape=tuple(jax.ShapeDtypeStruct((n, WIDTH), F32) for _ in range(8)),
        grid=(1,),
        in_specs=[tok(3 * WIDTH, 0), tok(4 * LANES, C_LORA // (4 * LANES)), tok(3 * WIDTH, 0), tok(4 * LANES, 0)]
        + [full(a) for a in wts],
        out_specs=(tok(WIDTH, 0),) * 8,
        compiler_params=_params(("arbitrary",), 32),
        name="rwkv_prep",
    )(p_main, p_main, prev_rkv, prev_lora, *wts)


def _post_math(y, bonus, g, lnw, lnb, ones_bd):
    mu = _headsum(y, ones_bd) * (1.0 / HEAD_DIM)
    d = y - mu
    var = _headsum(d * d, ones_bd) * (1.0 / HEAD_DIM)
    yn = d * lax.rsqrt(var + GN_EPS) * lnw + lnb
    return ((yn + bonus) * g).astype(BF16)


def _post_kernel(y_ref, bonus_ref, g_ref, lnw_ref, lnb_ref, ones_ref, o_ref):
    for p in range(PAIRS):
        sl = slice(p * LANES, (p + 1) * LANES)
        o_ref[:, sl] = _post_math(y_ref[:, sl], bonus_ref[:, sl], g_ref[:, sl], lnw_ref[:, sl],
                                  lnb_ref[:, sl], ones_ref[...])


def _post_call(y, bonus, g, lnw, lnb, ones_bd):
    return pl.pallas_call(
        _post_kernel,
        out_shape=jax.ShapeDtypeStruct(y.shape, BF16),
        name="rwkv_post",
    )(y, bonus, g, lnw, lnb, ones_bd)


CHUNK = 64


def _chunk_kernel(rkv_ref, lora_ref, *rest):
    prep_w = rest[:N_PREP_W]
    lnw_ref, lnb_ref, tri_ref, rw_ref, sfin_ref, s_ref, c_rkv, c_lora = rest[N_PREP_W:]
    ones_ref = prep_w[-1]
    c = pl.program_id(1)

    @pl.when(c == 0)
    def _():
        s_ref[...] = jnp.zeros_like(s_ref)
        c_rkv[...] = jnp.zeros_like(c_rkv)
        c_lora[...] = jnp.zeros_like(c_lora)

    z, zl = rkv_ref[...], lora_ref[...]
    zp, zlp = _shift_rows(z, c_rkv[...]), _shift_rows(zl, c_lora[...])
    c_rkv[...] = z[CHUNK - 1:CHUNK, :]
    c_lora[...] = zl[CHUNK - 1:CHUNK, :]
    lw, kap, bb, kt, r, v, bonus, g = _prep_math(z, zl, zp, zlp, prep_w)

    lane = lax.broadcasted_iota(I32, (1, LANES), 1)
    m0 = (lane < HEAD_DIM).astype(F32)
    m1 = 1.0 - m0
    rr = lax.broadcasted_iota(I32, (LANES, LANES), 0)
    cc = lax.broadcasted_iota(I32, (LANES, LANES), 1)
    strict = (cc & (CHUNK - 1)) < (rr & (CHUNK - 1))
    incl = (cc & (CHUNK - 1)) <= (rr & (CHUNK - 1))
    eye = (rr == cc).astype(F32)
    ones_bd = ones_ref[...]
    tri = tri_ref[...]

    def stack(x):
        return jnp.concatenate([x * m0, x * m1], axis=0)

    pairs = range(PAIRS)
    sls = [slice(p * LANES, (p + 1) * LANES) for p in pairs]
    n2 = 2 * CHUNK

    h1 = lw.astype(BF16)
    r1 = lw - h1.astype(F32)
    h2 = r1.astype(BF16)
    h3 = (r1 - h2.astype(F32)).astype(BF16)
    lam = _dot(tri, h1) + (_dot(tri, h2) + _dot(tri, h3))
    lam_c = lam[CHUNK - 1:CHUNK, :]
    e_in = jnp.exp(lam)
    e_ex = jnp.exp(lam - lw)
    e_inv = jnp.exp(-lam)
    e_rem = jnp.exp(lam_c - lam)
    g_c = jnp.exp(lam_c)
    kh, rh = kap * e_ex, r * e_in
    bc, kc = bb * e_inv, kt * e_inv
    bt, ktr = bb * e_rem, kt * e_rem
    s_kh = [stack(kh[:, sl]) for sl in sls]
    s_rh = [stack(rh[:, sl]) for sl in sls]
    s_v = [stack(v[:, sl]).astype(BF16) for sl in sls]
    bt_t = [stack(bt[:, sl]).T.astype(BF16) for sl in sls]
    kt_t = [stack(ktr[:, sl]).T.astype(BF16) for sl in sls]

    lhs = [jnp.concatenate([s_kh[p], s_rh[p]], axis=0).astype(BF16) for p in pairs]
    rhs = [jnp.concatenate([stack(bc[:, sl]), stack(kc[:, sl])], axis=0).astype(BF16) for sl in sls]
    gram = [_dot(lhs[p], rhs[p], _NT) for p in pairs]
    l_k = [jnp.where(strict, g[:n2, n2:], 0.0).astype(BF16) for g in gram]
    a_b = [jnp.where(incl, g[n2:, :n2], 0.0).astype(BF16) for g in gram]
    a_k = [jnp.where(incl, g[n2:, n2:], 0.0).astype(BF16) for g in gram]

    pw = [jnp.where(strict, -g[:n2, :n2], 0.0) for g in gram]
    tinv = [eye + x for x in pw]
    for _ in range(5):
        pwb = [x.astype(BF16) for x in pw]
        pw = [_dot(x, x) for x in pwb]
        tinv = [tinv[p] + _dot(tinv[p].astype(BF16), pw[p].astype(BF16)) for p in pairs]
    tinv = [x.astype(BF16) for x in tinv]

    lkv = [_dot(l_k[p], s_v[p]).astype(BF16) for p in pairs]
    wu = [_dot(tinv[p], jnp.concatenate([s_kh[p].astype(BF16), lkv[p]], axis=1)).astype(BF16) for p in pairs]
    mn = [_dot(bt_t[p], wu[p]) for p in pairs]
    ktv = [_dot(kt_t[p], s_v[p]) for p in pairs]
    ab_wu = [_dot(a_b[p], wu[p]) for p in pairs]
    akv = [_dot(a_k[p], s_v[p]) for p in pairs]
    m_mat = [eye * g_c[:, sls[p]] - mn[p][:, :LANES] for p in pairs]
    step_lhs = [jnp.concatenate([s_rh[p] - ab_wu[p][:, :LANES], m_mat[p]], axis=0).astype(BF16) for p in pairs]
    upd = [_dot(step_lhs[p], s_ref[p].astype(BF16)) for p in pairs]
    for p in pairs:
        s_ref[p] = upd[p][n2:] + (ktv[p] - mn[p][:, LANES:])
    for p in pairs:
        s_y = upd[p][:n2] + (akv[p] - ab_wu[p][:, LANES:])
        y = s_y[0:CHUNK, :] + s_y[CHUNK:2 * CHUNK, :]
        sl = sls[p]
        rw_ref[:, sl] = _post_math(y, bonus[:, sl], g[:, sl], lnw_ref[:, sl], lnb_ref[:, sl], ones_bd)

    @pl.when(c == pl.num_programs(1) - 1)
    def _():
        sfin_ref[0] = s_ref[...]


def _chunk_call(p_main, prep_w, lnw, lnb, tri, nb, t):
    nc = t // CHUNK
    tok = lambda n, cb: pl.BlockSpec((CHUNK, n), lambda b, c: (b * nc + c, cb))
    full = lambda a: pl.BlockSpec(a.shape, lambda b, c: (0,) * a.ndim)
    return pl.pallas_call(
        _chunk_kernel,
        out_shape=(jax.ShapeDtypeStruct((nb * t, WIDTH), BF16),
                   jax.ShapeDtypeStruct((nb, PAIRS, LANES, LANES), F32)),
        grid=(nb, nc),
        in_specs=[tok(3 * WIDTH, 0), tok(4 * LANES, C_LORA // (4 * LANES))]
        + [full(a) for a in prep_w] + [full(lnw), full(lnb), full(tri)],
        out_specs=(tok(WIDTH, 0), pl.BlockSpec((1, PAIRS, LANES, LANES), lambda b, c: (b, 0, 0, 0))),
        scratch_shapes=[pltpu.VMEM((PAIRS, LANES, LANES), F32), pltpu.VMEM((1, 3 * WIDTH), F32),
                        pltpu.VMEM((1, 4 * LANES), F32)],
        compiler_params=_params(("arbitrary", "arbitrary"), 48),
        name="rwkv_chunk_scan",
    )(p_main, p_main, *prep_w, lnw, lnb, tri)


def _row_to_cols(row):
    return jnp.broadcast_to(row, (LANES, LANES)).T


def _cols_to_row(cols):
    return jnp.broadcast_to(cols, (LANES, LANES)).T[0:1, :]


def _step_kernel(s_ref, logw_ref, kap_ref, bb_ref, kt_ref, r_ref, v_ref, so_ref, y_ref):
    s = s_ref[0]
    sa = -jnp.sum(s * kap_ref[0], axis=-1, keepdims=True)
    decayed = s * jnp.exp(logw_ref[0]) + sa * bb_ref[0]
    for p in range(PAIRS):
        v_cols = _row_to_cols(v_ref[0, :, p * LANES:(p + 1) * LANES])
        ys = []
        for hh in range(2):
            h = 2 * p + hh
            s_new = decayed[h] + v_cols[hh * HEAD_DIM:(hh + 1) * HEAD_DIM, 0:HEAD_DIM] * kt_ref[0, h]
            so_ref[0, h] = s_new
            ys.append(jnp.sum(s_new * r_ref[0, h], axis=-1, keepdims=True))
        y_ref[0, :, p * LANES:(p + 1) * LANES] = _cols_to_row(jnp.concatenate(ys, axis=0))


def _step_call(state, logw, kap, bb, kt, r, v):
    nb = state.shape[0]
    row = lambda x: x.reshape(nb, HEADS, 1, HEAD_DIM)
    s_spec = pl.BlockSpec((1, HEADS, HEAD_DIM, HEAD_DIM), lambda b: (b, 0, 0, 0))
    r_spec = pl.BlockSpec((1, HEADS, 1, HEAD_DIM), lambda b: (b, 0, 0, 0))
    w_spec = pl.BlockSpec((1, 1, WIDTH), lambda b: (b, 0, 0))
    so, y = pl.pallas_call(
        _step_kernel,
        out_shape=(jax.ShapeDtypeStruct(state.shape, F32), jax.ShapeDtypeStruct((nb, 1, WIDTH), F32)),
        grid=(nb,),
        in_specs=[s_spec] + [r_spec] * 5 + [w_spec],
        out_specs=(s_spec, w_spec),
        compiler_params=_params(("arbitrary",), 32),
        name="rwkv_step",
    )(state, row(logw), row(kap), row(bb), row(kt), row(r), v.reshape(nb, 1, WIDTH))
    return so, y.reshape(nb, WIDTH)


def _sortable(x):
    x = jnp.where(x == 0.0, 0.0, x)
    bits = pltpu.bitcast(x, I32)
    return jnp.where(bits < 0, bits ^ jnp.int32(0x7FFFFFFF), bits)


def _select_topk(keys_ref, score_of, valid_of, nkc, rows, kc_size, col_of, idx_bits):
    def lane_tiles(x, op):
        out = x[:, 0:LANES]
        for q in range(1, kc_size // LANES):
            out = op(out, x[:, q * LANES:(q + 1) * LANES])
        return out

    def count(pred):
        def body(kc, acc):
            return acc + lane_tiles(jnp.where(pred(kc), 1.0, 0.0), jnp.add)
        acc = lax.fori_loop(0, nkc, body, jnp.zeros((rows, LANES), F32))
        return jnp.sum(acc, axis=-1, keepdims=True)

    kf = float(TOPK)
    thr = jnp.where(count(lambda kc: keys_ref[kc] >= 0) >= kf, jnp.int32(0), jnp.int32(INT_MIN))

    def bit_body(it, thr):
        cand = thr + lax.shift_left(jnp.int32(1), jnp.int32(30) - it)
        return jnp.where(count(lambda kc: keys_ref[kc] >= cand) >= kf, cand, thr)

    thr = lax.fori_loop(0, 31, bit_body, thr)

    def tail_body(kc, carry):
        c_gt, c_eq, lo, hi = carry
        key, sc, ok = keys_ref[kc], score_of(kc), valid_of(kc)
        gt, eq = key > thr, key == thr
        c_gt = c_gt + lane_tiles(jnp.where(gt, 1.0, 0.0), jnp.add)
        c_eq = c_eq + lane_tiles(jnp.where(eq, 1.0, 0.0), jnp.add)
        lo = jnp.minimum(lo, lane_tiles(jnp.where((gt | eq) & ok, sc, jnp.inf), jnp.minimum))
        hi = jnp.maximum(hi, lane_tiles(jnp.where(ok & ~(gt | eq), sc, -jnp.inf), jnp.maximum))
        return c_gt, c_eq, lo, hi

    zeros = jnp.zeros((rows, LANES), F32)
    c_gt, c_eq, lo, hi = lax.fori_loop(0, nkc, tail_body, (zeros, zeros, zeros + jnp.inf, zeros - jnp.inf))
    need = kf - jnp.sum(c_gt, axis=-1, keepdims=True)
    split = (jnp.sum(c_eq, axis=-1, keepdims=True) > need) & (thr > jnp.int32(INT_MIN))
    any_split = jnp.max(jnp.where(split, 1.0, 0.0)) > 0.0

    def tie_break():
        def tie_body(it, x):
            cand = x + lax.shift_left(jnp.int32(1), jnp.int32(idx_bits - 1) - it)
            c = count(lambda kc: (keys_ref[kc] == thr) & (col_of(kc) <= cand))
            return jnp.where(c < need, cand, x)

        return lax.fori_loop(0, idx_bits, tie_body, jnp.full((rows, 1), -1, I32)) + 1

    jb = lax.cond(any_split, tie_break, lambda: jnp.full((rows, 1), INT_MAX, I32))

    t_sel = jnp.where(split, jnp.inf, jnp.min(lo, axis=-1, keepdims=True))
    t_uns = jnp.max(hi, axis=-1, keepdims=True)
    thr_bits = jnp.where(split, thr, jnp.int32(INT_MAX))

    def chosen(kc):
        sc, key = score_of(kc), keys_ref[kc]
        nearer = (sc - t_uns) > (t_sel - sc)
        by_bits = (key > thr_bits) | ((key == thr_bits) & (col_of(kc) <= jb))
        return (nearer | by_bits) & valid_of(kc)

    return chosen


TQ = 128
KC = 512


M_INIT = -1e30


def _idx_key_pack(kidx):
    hi, lo = _split(kidx)
    return jnp.concatenate([hi, lo, hi, jnp.zeros_like(hi)], axis=1)


def _attn_kernel(qidx_ref, widx_ref, kcat_ref, aq_ref, kt_ref, vt_ref,
                 att_ref, keys_ref, bias_ref, qcat_ref, d_ref, qm_ref, m_ref, l_ref, acc_ref):
    i = pl.program_id(1)
    nkc = lax.shift_right_logical((i + 1) * TQ + (KC - 1), int(np.log2(KC)))
    row = i * TQ + lax.broadcasted_iota(I32, (TQ, KC), 0)
    lane_kc = lax.broadcasted_iota(I32, (TQ, KC), 1)
    col_of = lambda kc: kc * KC + lane_kc
    head_rows = lambda h: slice(h * TQ, (h + 1) * TQ)
    first = lax.broadcasted_iota(I32, (1, LANES), 1) < HEAD_DIM

    for p in range(PAIRS):
        x = qidx_ref[:, p * LANES:(p + 1) * LANES]
        hi = x.astype(BF16).astype(F32)
        lo = x - hi
        hi_sw = pltpu.roll(hi, HEAD_DIM, 1)
        lo_sw = pltpu.roll(lo, HEAD_DIM, 1)
        for hh in range(2):
            rows = head_rows(2 * p + hh)
            own_hi, other_hi = (hi, hi_sw) if hh == 0 else (hi_sw, hi)
            qcat_ref[rows, 0:LANES] = jnp.where(first, own_hi, other_hi).astype(BF16)
            qcat_ref[rows, LANES:2 * LANES] = jnp.where(first, lo if hh == 0 else lo_sw, 0.0).astype(BF16)
    wv = widx_ref[:, IDX_DIM:IDX_DIM + HEADS] * (HEADS ** -0.5 * IDX_DIM ** -0.5)

    def score_body(kc, carry):
        d_ref[...] = _dot(qcat_ref[...], kcat_ref[kc])
        acc = jnp.zeros((TQ, KC), F32)
        for h in range(HEADS):
            acc = acc + jnp.maximum(d_ref[head_rows(h), :], 0.0) * wv[:, h:h + 1]
        keys_ref[kc] = jnp.where(col_of(kc) <= row, _sortable(acc), jnp.int32(INT_MIN))
        bias_ref[kc] = acc
        return carry

    lax.fori_loop(0, nkc, score_body, 0)

    chosen = _select_topk(keys_ref, lambda kc: bias_ref[kc], lambda kc: col_of(kc) <= row, nkc, TQ, KC, col_of, 12)

    def bias_body(kc, carry):
        bias_ref[kc] = jnp.where(chosen(kc), 0.0, 2.0 * M_INIT)
        return carry

    lax.fori_loop(0, nkc, bias_body, 0)

    for p in range(PAIRS):
        aq = aq_ref[:, p * LANES:(p + 1) * LANES] * (HEAD_DIM ** -0.5)
        qm_ref[p, 0:TQ, :] = jnp.where(first, aq, 0.0).astype(BF16)
        qm_ref[p, TQ:2 * TQ, :] = jnp.where(first, 0.0, aq).astype(BF16)
    def lane_tiles(x, op):
        out = x[:, 0:LANES]
        for q in range(1, KC // LANES):
            out = op(out, x[:, q * LANES:(q + 1) * LANES])
        return out

    halves = (slice(0, TQ), slice(TQ, 2 * TQ))

    m_ref[...] = jnp.full(m_ref.shape, M_INIT, F32)
    l_ref[...] = jnp.zeros(l_ref.shape, F32)
    acc_ref[...] = jnp.zeros(acc_ref.shape, F32)

    def pv_body(kc, carry):
        bias = bias_ref[kc]
        sls = [slice(p * LANES, (p + 1) * LANES) for p in range(PAIRS)]
        s = [_dot(qm_ref[p], kt_ref[kc, sls[p], :]) for p in range(PAIRS)]
        es = []
        for p in range(PAIRS):
            e = []
            for rs in halves:
                sh = s[p][rs] + bias
                m_old = m_ref[p, rs, :]
                m_new = jnp.maximum(m_old, jnp.max(lane_tiles(sh, jnp.maximum), axis=-1, keepdims=True))
                alpha = jnp.exp(m_old - m_new)
                x = jnp.exp(sh - m_new)
                m_ref[p, rs, :] = m_new
                l_ref[p, rs, :] = alpha * l_ref[p, rs, :] + lane_tiles(x, jnp.add)
                acc_ref[p, rs, :] = alpha * acc_ref[p, rs, :]
                e.append(x.astype(BF16))
            es.append(jnp.concatenate(e, axis=0))
        pv = [_dot(es[p], vt_ref[kc, sls[p], :], _NT) for p in range(PAIRS)]
        for p in range(PAIRS):
            acc_ref[p] += pv[p]
        return carry

    lax.fori_loop(0, nkc, pv_body, 0)

    for p in range(PAIRS):
        o = acc_ref[p] / jnp.sum(l_ref[p], axis=-1, keepdims=True)
        att_ref[:, p * LANES:(p + 1) * LANES] = jnp.where(first, o[0:TQ], o[TQ:2 * TQ]).astype(BF16)


def _attn_call(p_idx, kcat_t, p_main, kt, vt, nb, t):
    nq = t // TQ
    nkc = t // KC
    qtok = lambda n, cb: pl.BlockSpec((TQ, n), lambda b, i: (b * nq + i, cb))
    seq3 = lambda n: pl.BlockSpec((nkc, n, KC), lambda b, i: (b, 0, 0), pipeline_mode=pl.Buffered(1))
    return pl.pallas_call(
        _attn_kernel,
        out_shape=jax.ShapeDtypeStruct((nb * t, WIDTH), BF16),
        grid=(nb, nq),
        in_specs=[qtok(WIDTH, 0), qtok(LANES, WIDTH // LANES), seq3(4 * IDX_DIM),
                  qtok(WIDTH, C_AQ // WIDTH), seq3(WIDTH), seq3(WIDTH)],
        out_specs=qtok(WIDTH, 0),
        scratch_shapes=[pltpu.VMEM((nkc, TQ, KC), I32), pltpu.VMEM((nkc, TQ, KC), F32),
                        pltpu.VMEM((HEADS * TQ, 2 * LANES), BF16), pltpu.VMEM((HEADS * TQ, KC), F32),
                        pltpu.VMEM((PAIRS, 2 * TQ, LANES), BF16), pltpu.VMEM((PAIRS, 2 * TQ, 1), F32),
                        pltpu.VMEM((PAIRS, 2 * TQ, LANES), F32), pltpu.VMEM((PAIRS, 2 * TQ, LANES), F32)],
        compiler_params=_params(("arbitrary", "arbitrary"), 48),
        name="dsa_prompt_attention",
    )(p_idx, p_idx, kcat_t, p_main, kt, vt)


PG = 16
SROWS = 72
SCOLS = SROWS * PAGE


PG_IDX = 16


def _sidx_kernel(pt_ref, q_ref, w_ref, knew_ref, *rest, npages):
    pages, o_ref = rest[:PG_IDX], rest[PG_IDX]
    j = pl.program_id(1)
    qh, ql = _split(q_ref[0])
    w = w_ref[0] * (HEADS ** -0.5 * IDX_DIM ** -0.5)

    def score(keys):
        kh, kl = _split(keys)
        d = _dot(qh, kh) + (_dot(qh, kl) + _dot(ql, kh))
        return jnp.sum(jnp.maximum(d, 0.0) * w, axis=0, keepdims=True)

    @pl.when(j == 0)
    def _():
        o_ref[0] = jnp.zeros((SROWS, PAGE), F32)
        o_ref[0, npages:npages + 1, :] = score(knew_ref[0])

    sc = score(jnp.concatenate([pages[g][0] for g in range(PG_IDX)], axis=1))
    for g in range(PG_IDX):
        o_ref[0, pl.ds(j * PG_IDX + g, 1), :] = sc[:, g * PAGE:(g + 1) * PAGE]


def _sidx_call(page_table, q, w, knew, cache_kidx):
    nb, npages = page_table.shape
    one = lambda s: pl.BlockSpec((1,) + s, lambda b, j, pt: (b, 0, 0))
    page = lambda g: pl.BlockSpec((1, IDX_DIM, PAGE), lambda b, j, pt: (pt[b, j * PG_IDX + g], 0, 0))
    return pl.pallas_call(
        functools.partial(_sidx_kernel, npages=npages),
        out_shape=jax.ShapeDtypeStruct((nb, SROWS, PAGE), F32),
        grid_spec=pltpu.PrefetchScalarGridSpec(
            num_scalar_prefetch=1,
            grid=(nb, npages // PG_IDX),
            in_specs=[one((HEADS, IDX_DIM)), one((HEADS, 1)), one((IDX_DIM, PAGE))]
            + [page(g) for g in range(PG_IDX)],
            out_specs=one((SROWS, PAGE)),
        ),
        compiler_params=_params(("arbitrary", "arbitrary"), 32),
        name="dsa_sample_index_scores",
    )(page_table, q, w, knew, *([cache_kidx] * PG_IDX))


SKC = 512
SNKC = SCOLS // SKC


def _ssel_kernel(sc_ref, bias_ref, keys_ref, chunks_ref, *, n_valid):
    rows = sc_ref.shape[0]
    lane_kc = lax.broadcasted_iota(I32, (rows, SKC), 1)
    col_of = lambda kc: kc * SKC + lane_kc
    valid_of = lambda kc: col_of(kc) < n_valid
    for c in range(SNKC):
        x = sc_ref[:, c * SKC:(c + 1) * SKC]
        chunks_ref[c] = x
        keys_ref[c] = jnp.where(valid_of(c), _sortable(x), jnp.int32(INT_MIN))
    chosen = _select_topk(keys_ref, lambda kc: chunks_ref[kc], valid_of, SNKC, rows, SKC, col_of, 14)
    for c in range(SNKC):
        bias_ref[:, c * SKC:(c + 1) * SKC] = jnp.where(chosen(c), 0.0, NEG)


def _ssel_call(scores, n_valid):
    rows = scores.shape[0]
    return pl.pallas_call(
        functools.partial(_ssel_kernel, n_valid=n_valid),
        out_shape=jax.ShapeDtypeStruct(scores.shape, F32),
        scratch_shapes=[pltpu.VMEM((SNKC, rows, SKC), I32), pltpu.VMEM((SNKC, rows, SKC), F32)],
        name="dsa_sample_select",
    )(scores)


def _sattn_kernel(pt_ref, q_ref, kn_ref, vn_ref, bias_ref, *rest, npages):
    kpages, vpages = rest[:PG], rest[PG:2 * PG]
    o_ref, qb_ref, m_ref, l_ref, acc_ref = rest[2 * PG:]
    j = pl.program_id(1)
    scale = HEAD_DIM ** -0.5

    @pl.when(j == 0)
    def _():
        on = bias_ref[0, npages:npages + 1, 0:1] == 0.0
        first = on & (lax.broadcasted_iota(I32, (1, PAGE), 1) == 0)
        for p in range(PAIRS):
            sl = slice(p * LANES, (p + 1) * LANES)
            q_cols = _row_to_cols(q_ref[0, :, sl] * scale)
            k_cols = _row_to_cols(kn_ref[0, :, sl])
            v_cols = _row_to_cols(vn_ref[0, :, sl])
            for hh in range(2):
                h = 2 * p + hh
                rows = slice(hh * HEAD_DIM, (hh + 1) * HEAD_DIM)
                qb_ref[h] = q_cols[rows]
                s_new = jnp.sum(q_cols[rows] * k_cols[rows], axis=0, keepdims=True)
                m_ref[h] = jnp.where(on, s_new, NEG)
                l_ref[h] = jnp.broadcast_to(jnp.where(on, 1.0, 0.0), (1, PAGE))
                acc_ref[h] = jnp.where(first, v_cols[rows], 0.0)

    for g in range(PG):
        on = bias_ref[0, pl.ds(j * PG + g, 1), :] == 0.0
        for h in range(HEADS):
            s = jnp.sum(qb_ref[h] * kpages[g][0, h], axis=0, keepdims=True)
            m = m_ref[h]
            m_new = jnp.maximum(m, jnp.max(jnp.where(on, s, NEG), axis=-1, keepdims=True))
            e = jnp.where(on, jnp.exp(s - m_new), 0.0)
            alpha = jnp.exp(m - m_new)
            m_ref[h] = m_new
            l_ref[h] = alpha * l_ref[h] + jnp.sum(e, axis=-1, keepdims=True)
            acc_ref[h] = alpha * acc_ref[h] + e * vpages[g][0, h]

    @pl.when(j == pl.num_programs(1) - 1)
    def _():
        for p in range(PAIRS):
            outs = [jnp.sum(acc_ref[h], axis=-1, keepdims=True) / l_ref[h][:, 0:1] for h in (2 * p, 2 * p + 1)]
            o_ref[0, :, p * LANES:(p + 1) * LANES] = _cols_to_row(jnp.concatenate(outs, axis=0))


def _sattn_call(page_table, q, kn, vn, bias, ck, cv):
    nb, npages = page_table.shape
    one = lambda s: pl.BlockSpec((1,) + s, lambda b, j, pt: (b,) + (0,) * len(s))
    col = one((1, WIDTH))
    page = lambda g: pl.BlockSpec((1, HEADS, HEAD_DIM, PAGE), lambda b, j, pt: (pt[b, j * PG + g], 0, 0, 0))
    return pl.pallas_call(
        functools.partial(_sattn_kernel, npages=npages),
        out_shape=jax.ShapeDtypeStruct((nb, 1, WIDTH), F32),
        grid_spec=pltpu.PrefetchScalarGridSpec(
            num_scalar_prefetch=1,
            grid=(nb, npages // PG),
            in_specs=[col, col, col, one((SROWS, PAGE))] + [page(g) for g in range(PG)] * 2,
            out_specs=col,
            scratch_shapes=[pltpu.VMEM((HEADS, HEAD_DIM, PAGE), F32), pltpu.VMEM((HEADS, 1, PAGE), F32),
                            pltpu.VMEM((HEADS, 1, PAGE), F32), pltpu.VMEM((HEADS, HEAD_DIM, PAGE), F32)],
        ),
        compiler_params=_params(("arbitrary", "arbitrary"), 48),
        name="dsa_sample_attention",
    )(page_table, q, kn, vn, bias, *([ck] * PG), *([cv] * PG))


def _merge_kernel(rw_ref, att_ref, wr_ref, wa_ref, ga_ref, gb_ref, o_ref):
    yr = _dot(rw_ref[...], wr_ref[...])
    ya = _dot(att_ref[...], wa_ref[...])
    o_ref[...] = (_sigmoid(ga_ref[...]) * yr + _sigmoid(gb_ref[...]) * ya).astype(BF16)


def _merge_call(rw, att, w_r, w_a, p_main, tm, tn):
    m = rw.shape[0]
    lhs = pl.BlockSpec((tm, WIDTH), lambda j, i: (i, 0))
    rhs = pl.BlockSpec((WIDTH, tn), lambda j, i: (0, j))
    gate = lambda c0: pl.BlockSpec((tm, tn), lambda j, i: (i, c0 // tn + j))
    return pl.pallas_call(
        _merge_kernel,
        out_shape=jax.ShapeDtypeStruct((m, D_MODEL), BF16),
        grid=(D_MODEL // tn, m // tm),
        in_specs=[lhs, lhs, rhs, rhs, gate(C_GA), gate(C_GB)],
        out_specs=pl.BlockSpec((tm, tn), lambda j, i: (i, j)),
        compiler_params=_params(("arbitrary", "arbitrary"), 32),
        name="mixer_merge",
    )(rw, att, w_r, w_a, p_main, p_main)


def _rms(x, g):
    return x * lax.rsqrt(jnp.mean(x * x, axis=-1, keepdims=True) + RMS_EPS) * g


def _outproj_kernel(m_ref, w_ref, x_ref, gpost_ref, gpre_ref, h_ref, f_ref):
    h = x_ref[...] + _rms(_dot(m_ref[...], w_ref[...]), gpost_ref[...])
    h_ref[...] = h
    f_ref[...] = _rms(h, gpre_ref[...]).astype(BF16)


def _outproj_call(mix, w_out, x, g_post, g_pre, tm):
    m = x.shape[0]
    tok = pl.BlockSpec((tm, D_MODEL), lambda i: (i, 0))
    full = lambda a: pl.BlockSpec(a.shape, lambda i: (0, 0))
    return pl.pallas_call(
        _outproj_kernel,
        out_shape=(jax.ShapeDtypeStruct((m, D_MODEL), F32), jax.ShapeDtypeStruct((m, D_MODEL), BF16)),
        grid=(m // tm,),
        in_specs=[tok, full(w_out), tok, full(g_post), full(g_pre)],
        out_specs=(tok, tok),
        compiler_params=_params(("arbitrary",), 40),
        name="mixer_out_proj",
    )(mix, w_out, x, g_post, g_pre)


def _ffn_kernel(f_ref, wg_ref, wu_ref, wd_ref, h_ref, gp_ref, y_ref, acc_ref):
    j = pl.program_id(1)

    @pl.when(j == 0)
    def _():
        acc_ref[...] = jnp.zeros(acc_ref.shape, F32)

    f = f_ref[...]
    gate = _dot(f, wg_ref[...])
    t = (gate * _sigmoid(gate) * _dot(f, wu_ref[...])).astype(BF16)
    acc_ref[...] += _dot(t, wd_ref[...])

    @pl.when(j == pl.num_programs(1) - 1)
    def _():
        y_ref[...] = h_ref[...] + _rms(acc_ref[...], gp_ref[...])


def _ffn_call(f, w_gate, w_up, w_down, h, g_post, tm, tf):
    m = f.shape[0]
    d_ff = w_gate.shape[1]
    tok = pl.BlockSpec((tm, D_MODEL), lambda i, j: (i, 0))
    return pl.pallas_call(
        _ffn_kernel,
        out_shape=jax.ShapeDtypeStruct((m, D_MODEL), F32),
        grid=(m // tm, d_ff // tf),
        in_specs=[tok, pl.BlockSpec((D_MODEL, tf), lambda i, j: (0, j)), pl.BlockSpec((D_MODEL, tf), lambda i, j: (0, j)),
                  pl.BlockSpec((tf, D_MODEL), lambda i, j: (j, 0)), tok, pl.BlockSpec((1, D_MODEL), lambda i, j: (0, 0))],
        out_specs=tok,
        scratch_shapes=[pltpu.VMEM((tm, D_MODEL), F32)],
        compiler_params=_params(("arbitrary", "arbitrary"), 48),
        name="swiglu_ffn",
    )(f, w_gate, w_up, w_down, h, g_post)


def _pad_cols(a, n):
    return jnp.pad(a, ((0, 0), (0, n - a.shape[1])))


def _regroup_cols(a):
    w3, ls = 3 * WIDTH, LORA_SMALL
    lora = jnp.concatenate([_pad_cols(a[:, w3:w3 + ls], LANES), _pad_cols(a[:, w3 + ls:w3 + 2 * ls], LANES),
                            a[:, w3 + 2 * ls:]], axis=1)
    return a[:, :w3], lora


def _ungroup_cols(rkv, lora):
    ls = LORA_SMALL
    return jnp.concatenate([rkv, lora[:, :ls], lora[:, LANES:LANES + ls], lora[:, 2 * LANES:]], axis=1)


def kernel(x_prompt, x_sample, cache_k, cache_v, cache_kidx, state_wkv, state_shift, page_table, g_pre_mix, w_in,
           rwkv_mu, w0, w_w2, a0, w_a2, w_g2, k_k, k_a, r_k, ln_x_w, ln_x_b, w_o_rwkv, w_o_attn, w_out,
           g_post_mix, g_pre_ffn, w_gate, w_up, w_down, g_post_ffn):
    nb, t, _ = x_prompt.shape
    ns = x_sample.shape[0]
    row = lambda a: a.reshape(1, -1).astype(F32)

    o_q = RWKV_COLS
    o_qi = o_q + 3 * WIDTH
    o_ki = o_qi + WIDTH
    o_ga = o_ki + IDX_DIM + HEADS
    w_rkv, w_lora = _regroup_cols(w_in[:, :RWKV_COLS])
    o_k = o_q + WIDTH
    w_main = jnp.concatenate([w_rkv, w_in[:, o_q:o_k], w_in[:, o_ga:], w_lora], axis=1).astype(BF16)
    w_kv = w_in[:, o_k:o_qi].astype(BF16)
    w_kt, w_vt = w_kv[:, :WIDTH].T, w_kv[:, WIDTH:].T
    w_idx_hi, w_idx_lo = _split(jnp.concatenate([w_in[:, o_qi:o_ki], _pad_cols(w_in[:, o_ki:o_ga], LANES)], axis=1))
    mu_rkv, mu_lora = _regroup_cols(row(rwkv_mu))
    pad_rows = lambda a: jnp.pad(a, ((0, LANES - a.shape[0]), (0, 0))).astype(BF16)
    lane = np.arange(LANES)
    ones_bd = jnp.asarray((lane[:, None] // HEAD_DIM) == (lane[None, :] // HEAD_DIM), BF16)
    tri = jnp.asarray(np.arange(CHUNK)[:, None] >= np.arange(CHUNK)[None, :], BF16)
    prep_w = [mu_rkv, mu_lora, row(w0), row(a0), row(k_k), row(k_a), row(r_k), pad_rows(w_w2), pad_rows(w_a2),
              w_g2.astype(BF16), ones_bd]
    lnw, lnb = row(ln_x_w), row(ln_x_b)
    w_or, w_oa, w_o = w_o_rwkv.astype(BF16), w_o_attn.astype(BF16), w_out.astype(BF16)
    w_g, w_u, w_d = w_gate.astype(BF16), w_up.astype(BF16), w_down.astype(BF16)
    gains = [row(g_pre_mix), row(g_post_mix), row(g_pre_ffn), row(g_post_ffn)]

    def project(x2d, tm, transposed):
        u = _rms_call(x2d, gains[0], min(tm, 512), transposed)
        p_main = _mm_call([u[0]], [w_main], [(0, 0)], F32, min(tm, 512), N_MAIN // 4, "in_proj")
        p_idx = _mm_call([u[0], u[1]], [w_idx_hi, w_idx_lo], [(0, 0), (0, 1), (1, 0)], F32, tm, 384, "in_proj_idx")
        return p_main, p_idx, u

    def finish(x2d, p_main, rw, att, tm, tn, tm_ffn):
        mix = _merge_call(rw, att, w_or, w_oa, p_main, tm, tn)
        h, f = _outproj_call(mix, w_o, x2d, gains[1], gains[2], min(tm, 256))
        return _ffn_call(f, w_g, w_u, w_d, h, gains[3], tm_ffn, 512)

    m = nb * t
    xp = x_prompt.reshape(m, D_MODEL)
    p_main, p_idx, u_p = project(xp, 1024, True)
    rw, s_fin = _chunk_call(p_main, prep_w, lnw, lnb, tri, nb, t)
    kidx = p_idx[:, WIDTH:WIDTH + IDX_DIM]
    k_t, k_tc = _proj_t_call(w_kt, u_p[2], nb, t, KC)
    v_t, v_tc = _proj_t_call(w_vt, u_p[2], nb, t, KC)
    chunk_t = lambda a: jnp.swapaxes(a.reshape(m // KC, KC, a.shape[1]), 1, 2)
    att = _attn_call(p_idx, chunk_t(_idx_key_pack(kidx)), p_main, k_tc, v_tc, nb, t)
    tok_major = lambda a: jnp.transpose(a.reshape(nb, HEADS, HEAD_DIM, t), (0, 3, 1, 2))
    k_p, v_p = tok_major(k_t), tok_major(v_t)
    y_p = finish(xp, p_main, rw, att, 1024, 512, 512)
    blocks = jnp.stack([s_fin[:, :, :HEAD_DIM, :HEAD_DIM], s_fin[:, :, HEAD_DIM:, HEAD_DIM:]], axis=2)
    wkv_p = jnp.swapaxes(blocks.reshape(nb, HEADS, HEAD_DIM, HEAD_DIM), -1, -2)
    last = p_main.reshape(nb, t, N_MAIN)[:, -1]
    shift_p = _ungroup_cols(last[:, :3 * WIDTH], last[:, C_LORA:])

    xs = x_sample.reshape(ns, D_MODEL)
    ps_main, ps_idx, u_s = project(xs, ns, False)
    ps_kv = _mm_call([u_s[0]], [w_kv], [(0, 0)], F32, ns, 512, "in_proj_kv")
    prev_rkv, prev_lora = _regroup_cols(state_shift)
    sf = _prep_call(ps_main, prev_rkv, prev_lora, lax.optimization_barrier(prep_w))
    logw_s, kap_s, bb_s, kt_s, r_s, v_s, bonus_s, g_s = sf
    wkv_s, y_s = _step_call(state_wkv, logw_s, kap_s, bb_s, kt_s, r_s, v_s)
    rw_s = _post_call(y_s, bonus_s, g_s, lnw, lnb, ones_bd)

    npages = page_table.shape[1]
    past = npages * PAGE
    q_i = ps_idx[:, :WIDTH].reshape(ns, HEADS, IDX_DIM)
    w_i = ps_idx[:, WIDTH + IDX_DIM:WIDTH + IDX_DIM + HEADS].reshape(ns, HEADS, 1)
    kidx_s = ps_idx[:, WIDTH:WIDTH + IDX_DIM]
    knew = jnp.pad(kidx_s[:, :, None], ((0, 0), (0, 0), (0, PAGE - 1)))
    scores = _sidx_call(page_table, q_i, w_i, knew, jnp.transpose(cache_kidx, (0, 2, 1)))
    bias = _ssel_call(scores.reshape(ns, SCOLS), past + 1).reshape(ns, SROWS, PAGE)
    k_s, v_s_att = ps_kv[:, :WIDTH], ps_kv[:, WIDTH:]
    seq = lambda a: a.reshape(ns, 1, WIDTH)
    att_s = _sattn_call(page_table, seq(ps_main[:, C_AQ:C_AQ + WIDTH]), seq(k_s), seq(v_s_att), bias,
                        jnp.transpose(cache_k, (0, 2, 3, 1)), jnp.transpose(cache_v, (0, 2, 3, 1)))
    y_smp = finish(xs, ps_main, rw_s, att_s.reshape(ns, WIDTH).astype(BF16), ns, 512, ns)
    shift_s = _ungroup_cols(ps_main[:, :3 * WIDTH], ps_main[:, C_LORA:])

    hd = (HEADS, HEAD_DIM)
    return (y_p.reshape(nb, t, D_MODEL), y_smp.reshape(ns, 1, D_MODEL),
            k_p, v_p, kidx.reshape(nb, t, IDX_DIM),
            wkv_p, shift_p,
            k_s.reshape(ns, 1, *hd), v_s_att.reshape(ns, 1, *hd), kidx_s.reshape(ns, 1, IDX_DIM),
            wkv_s, shift_s)
```

```python
import functools

import numpy as np
import jax
import jax.numpy as jnp
from jax import lax
from jax.experimental import pallas as pl
from jax.experimental.pallas import tpu as pltpu

F32, BF16, I32 = jnp.float32, jnp.bfloat16, jnp.int32

D_MODEL = 2048
HEADS = 16
HEAD_DIM = 64
WIDTH = HEADS * HEAD_DIM
LORA_SMALL = 96
LORA_GATE = 256
RWKV_COLS = 3 * WIDTH + 2 * LORA_SMALL + LORA_GATE
IDX_DIM = 64
TOPK = 256
PAGE = 128
RMS_EPS = 1e-6
GN_EPS = 64e-5

LANES = 128
PAIRS = WIDTH // LANES

C_AQ, C_GA, C_GB, C_LORA = 3 * WIDTH, 4 * WIDTH, 4 * WIDTH + D_MODEL, 4 * WIDTH + 2 * D_MODEL
N_MAIN = C_LORA + 4 * LANES

INT_MIN = int(np.iinfo(np.int32).min)
INT_MAX = int(np.iinfo(np.int32).max)
NEG = -1e30

_NN = (((1,), (0,)), ((), ()))
_NT = (((1,), (1,)), ((), ()))


def _params(sem, vmem_mb):
    return pltpu.CompilerParams(dimension_semantics=sem, vmem_limit_bytes=vmem_mb * 1024 * 1024)


def _dot(a, b, dims=_NN):
    return lax.dot_general(a, b, dims, preferred_element_type=F32)


def _split(x):
    hi = x.astype(BF16)
    return hi, (x - hi.astype(F32)).astype(BF16)


def _sigmoid(x):
    return 1.0 / (1.0 + jnp.exp(-x))


def _headsum(x, ones_bd):
    hi, lo = _split(x)
    return _dot(hi, ones_bd) + _dot(lo, ones_bd)


def _headsum_full(x, ones_bd):
    return jnp.concatenate(
        [_headsum(x[:, p * LANES:(p + 1) * LANES], ones_bd) for p in range(PAIRS)], axis=1)


def _rms_kernel(x_ref, g_ref, hi_ref, lo_ref, *t_ref):
    x = x_ref[...]
    y = x * lax.rsqrt(jnp.mean(x * x, axis=-1, keepdims=True) + RMS_EPS) * g_ref[...]
    hi = y.astype(BF16)
    hi_ref[...] = hi
    lo_ref[...] = (y - hi.astype(F32)).astype(BF16)
    if t_ref:
        t_ref[0][...] = hi.astype(F32).T.astype(BF16)


def _rms_call(x, g, tm, transposed=False):
    m, d = x.shape
    spec = pl.BlockSpec((tm, d), lambda i: (i, 0))
    shapes = [jax.ShapeDtypeStruct((m, d), BF16), jax.ShapeDtypeStruct((m, d), BF16)]
    specs = [spec, spec]
    if transposed:
        shapes.append(jax.ShapeDtypeStruct((d, m), BF16))
        specs.append(pl.BlockSpec((d, tm), lambda i: (0, i)))
    return pl.pallas_call(
        _rms_kernel,
        out_shape=tuple(shapes),
        grid=(m // tm,),
        in_specs=[spec, pl.BlockSpec((1, d), lambda i: (0, 0))],
        out_specs=tuple(specs),
        compiler_params=_params(("arbitrary",), 40),
        name="rms_norm",
    )(x, g)


def _proj_t_kernel(w_ref, ut_ref, o32_ref, o16_ref):
    y = _dot(w_ref[...], ut_ref[...])
    o32_ref[0] = y
    o16_ref[0] = y.astype(BF16)


def _proj_t_call(w_t, u_t, nb, t, tn):
    n, d = w_t.shape
    nc = t // tn
    return pl.pallas_call(
        _proj_t_kernel,
        out_shape=(jax.ShapeDtypeStruct((nb, n, t), F32), jax.ShapeDtypeStruct((nb * nc, n, tn), BF16)),
        grid=(nb, nc),
        in_specs=[pl.BlockSpec((n, d), lambda b, j: (0, 0)), pl.BlockSpec((d, tn), lambda b, j: (0, b * nc + j))],
        out_specs=(pl.BlockSpec((1, n, tn), lambda b, j: (b, 0, j)),
                   pl.BlockSpec((1, n, tn), lambda b, j: (b * nc + j, 0, 0))),
        compiler_params=_params(("arbitrary", "arbitrary"), 40),
        name="in_proj_t",
    )(w_t, u_t)


def _mm_kernel(*refs, na, nb, terms):
    a, b, o = refs[:na], refs[na:na + nb], refs[na + nb]
    acc = None
    for ia, ib in terms:
        d = _dot(a[ia][...], b[ib][...])
        acc = d if acc is None else acc + d
    o[...] = acc.astype(o.dtype)


def _mm_call(a_list, b_list, terms, out_dtype, tm, tn, name):
    m, k = a_list[0].shape
    n = b_list[0].shape[1]
    a_spec = pl.BlockSpec((tm, k), lambda j, i: (i, 0))
    b_spec = pl.BlockSpec((k, tn), lambda j, i: (0, j))
    return pl.pallas_call(
        functools.partial(_mm_kernel, na=len(a_list), nb=len(b_list), terms=terms),
        out_shape=jax.ShapeDtypeStruct((m, n), out_dtype),
        grid=(n // tn, m // tm),
        in_specs=[a_spec] * len(a_list) + [b_spec] * len(b_list),
        out_specs=pl.BlockSpec((tm, tn), lambda j, i: (i, j)),
        compiler_params=_params(("arbitrary", "arbitrary"), 48),
        name=name,
    )(*a_list, *b_list)


def _shift_rows(z, first_row):
    rows = lax.broadcasted_iota(I32, z.shape, 0)
    return jnp.where(rows == 0, first_row, pltpu.roll(z, 1, 0))


N_PREP_W = 11


def _prep_math(z, zl, zp, zlp, w):
    mu_rkv, mu_lora, w0, a0, k_k, k_a, r_k, w_w2, w_a2, w_g2, ones = [x[...] for x in w]
    zs = z + (zp - z) * mu_rkv
    zls = zl + (zlp - zl) * mu_lora
    r, k, v = zs[:, 0:WIDTH], zs[:, WIDTH:2 * WIDTH], zs[:, 2 * WIDTH:3 * WIDTH]
    wd, ad, gd = zls[:, 0:LANES], zls[:, LANES:2 * LANES], zls[:, 2 * LANES:4 * LANES]
    nlw = -(w0 + _dot(jnp.tanh(wd).astype(BF16), w_w2))
    softplus = jnp.maximum(nlw, 0.0) + jnp.log(1.0 + jnp.exp(-jnp.abs(nlw)))
    logw = -jnp.exp(-softplus - 0.5)
    a = _sigmoid(a0 + _dot(ad.astype(BF16), w_a2))
    g = _dot(_sigmoid(gd).astype(BF16), w_g2)
    kkv = k * k_k
    kap = kkv / jnp.maximum(jnp.sqrt(_headsum_full(kkv * kkv, ones)), 1e-12)
    kt = k * (1.0 + (a - 1.0) * k_a)
    bonus = _headsum_full(r * kt * r_k, ones) * v
    return logw, kap, kap * a, kt, r, v, bonus, g


def _prep_kernel(rkv_ref, lora_ref, p_rkv_ref, p_lora_ref, *rest):
    w, outs = rest[:N_PREP_W], rest[N_PREP_W:]
    vals = _prep_math(rkv_ref[...], lora_ref[...], p_rkv_ref[...], p_lora_ref[...], w)
    for o, x in zip(outs, vals):
        o[...] = x


def _prep_call(p_main, prev_rkv, prev_lora, wts):
    n = p_main.shape[0]
    tok = lambda c, cb: pl.BlockSpec((n, c), lambda i: (0, cb))
    full = lambda a: pl.BlockSpec(a.shape, lambda i: (0,) * a.ndim)
    return pl.pallas_call(
        _prep_kernel,
        out_shape=tuple(jax.ShapeDtypeStruct((n, WIDTH), F32) for _ in range(8)),
        grid=(1,),
        in_specs=[tok(3 * WIDTH, 0), tok(4 * LANES, C_LORA // (4 * LANES)), tok(3 * WIDTH, 0), tok(4 * LANES, 0)]
        + [full(a) for a in wts],
        out_specs=(tok(WIDTH, 0),) * 8,
        compiler_params=_params(("arbitrary",), 32),
        name="rwkv_prep",
    )(p_main, p_main, prev_rkv, prev_lora, *wts)


def _post_math(y, bonus, g, lnw, lnb, ones_bd):
    mu = _headsum(y, ones_bd) * (1.0 / HEAD_DIM)
    d = y - mu
    var = _headsum(d * d, ones_bd) * (1.0 / HEAD_DIM)
    yn = d * lax.rsqrt(var + GN_EPS) * lnw + lnb
    return ((yn + bonus) * g).astype(BF16)


def _post_kernel(y_ref, bonus_ref, g_ref, lnw_ref, lnb_ref, ones_ref, o_ref):
    for p in range(PAIRS):
        sl = slice(p * LANES, (p + 1) * LANES)
        o_ref[:, sl] = _post_math(y_ref[:, sl], bonus_ref[:, sl], g_ref[:, sl], lnw_ref[:, sl],
                                  lnb_ref[:, sl], ones_ref[...])


def _post_call(y, bonus, g, lnw, lnb, ones_bd):
    return pl.pallas_call(
        _post_kernel,
        out_shape=jax.ShapeDtypeStruct(y.shape, BF16),
        name="rwkv_post",
    )(y, bonus, g, lnw, lnb, ones_bd)


CHUNK = 64


def _chunk_kernel(rkv_ref, lora_ref, *rest):
    prep_w = rest[:N_PREP_W]
    lnw_ref, lnb_ref, tri_ref, rw_ref, sfin_ref, s_ref, c_rkv, c_lora = rest[N_PREP_W:]
    ones_ref = prep_w[-1]
    c = pl.program_id(1)

    @pl.when(c == 0)
    def _():
        s_ref[...] = jnp.zeros_like(s_ref)
        c_rkv[...] = jnp.zeros_like(c_rkv)
        c_lora[...] = jnp.zeros_like(c_lora)

    z, zl = rkv_ref[...], lora_ref[...]
    zp, zlp = _shift_rows(z, c_rkv[...]), _shift_rows(zl, c_lora[...])
    c_rkv[...] = z[CHUNK - 1:CHUNK, :]
    c_lora[...] = zl[CHUNK - 1:CHUNK, :]
    lw, kap, bb, kt, r, v, bonus, g = _prep_math(z, zl, zp, zlp, prep_w)

    lane = lax.broadcasted_iota(I32, (1, LANES), 1)
    m0 = (lane < HEAD_DIM).astype(F32)
    m1 = 1.0 - m0
    rr = lax.broadcasted_iota(I32, (LANES, LANES), 0)
    cc = lax.broadcasted_iota(I32, (LANES, LANES), 1)
    strict = (cc & (CHUNK - 1)) < (rr & (CHUNK - 1))
    incl = (cc & (CHUNK - 1)) <= (rr & (CHUNK - 1))
    eye = (rr == cc).astype(F32)
    ones_bd = ones_ref[...]
    tri = tri_ref[...]

    def stack(x):
        return jnp.concatenate([x * m0, x * m1], axis=0)

    pairs = range(PAIRS)
    sls = [slice(p * LANES, (p + 1) * LANES) for p in pairs]
    n2 = 2 * CHUNK

    h1 = lw.astype(BF16)
    r1 = lw - h1.astype(F32)
    h2 = r1.astype(BF16)
    h3 = (r1 - h2.astype(F32)).astype(BF16)
    lam = _dot(tri, h1) + (_dot(tri, h2) + _dot(tri, h3))
    lam_c = lam[CHUNK - 1:CHUNK, :]
    e_in = jnp.exp(lam)
    e_ex = jnp.exp(lam - lw)
    e_inv = jnp.exp(-lam)
    e_rem = jnp.exp(lam_c - lam)
    g_c = jnp.exp(lam_c)
    kh, rh = kap * e_ex, r * e_in
    bc, kc = bb * e_inv, kt * e_inv
    bt, ktr = bb * e_rem, kt * e_rem
    s_kh = [stack(kh[:, sl]) for sl in sls]
    s_rh = [stack(rh[:, sl]) for sl in sls]
    s_v = [stack(v[:, sl]).astype(BF16) for sl in sls]
    bt_t = [stack(bt[:, sl]).T.astype(BF16) for sl in sls]
    kt_t = [stack(ktr[:, sl]).T.astype(BF16) for sl in sls]

    lhs = [jnp.concatenate([s_kh[p], s_rh[p]], axis=0).astype(BF16) for p in pairs]
    rhs = [jnp.concatenate([stack(bc[:, sl]), stack(kc[:, sl])], axis=0).astype(BF16) for sl in sls]
    gram = [_dot(lhs[p], rhs[p], _NT) for p in pairs]
    l_k = [jnp.where(strict, g[:n2, n2:], 0.0).astype(BF16) for g in gram]
    a_b = [jnp.where(incl, g[n2:, :n2], 0.0).astype(BF16) for g in gram]
    a_k = [jnp.where(incl, g[n2:, n2:], 0.0).astype(BF16) for g in gram]

    pw = [jnp.where(strict, -g[:n2, :n2], 0.0) for g in gram]
    tinv = [eye + x for x in pw]
    for _ in range(5):
        pwb = [x.astype(BF16) for x in pw]
        pw = [_dot(x, x) for x in pwb]
        tinv = [tinv[p] + _dot(tinv[p].astype(BF16), pw[p].astype(BF16)) for p in pairs]
    tinv = [x.astype(BF16) for x in tinv]

    lkv = [_dot(l_k[p], s_v[p]).astype(BF16) for p in pairs]
    wu = [_dot(tinv[p], jnp.concatenate([s_kh[p].astype(BF16), lkv[p]], axis=1)).astype(BF16) for p in pairs]
    mn = [_dot(bt_t[p], wu[p]) for p in pairs]
    ktv = [_dot(kt_t[p], s_v[p]) for p in pairs]
    ab_wu = [_dot(a_b[p], wu[p]) for p in pairs]
    akv = [_dot(a_k[p], s_v[p]) for p in pairs]
    m_mat = [eye * g_c[:, sls[p]] - mn[p][:, :LANES] for p in pairs]
    step_lhs = [jnp.concatenate([s_rh[p] - ab_wu[p][:, :LANES], m_mat[p]], axis=0).astype(BF16) for p in pairs]
    upd = [_dot(step_lhs[p], s_ref[p].astype(BF16)) for p in pairs]
    for p in pairs:
        s_ref[p] = upd[p][n2:] + (ktv[p] - mn[p][:, LANES:])
    for p in pairs:
        s_y = upd[p][:n2] + (akv[p] - ab_wu[p][:, LANES:])
        y = s_y[0:CHUNK, :] + s_y[CHUNK:2 * CHUNK, :]
        sl = sls[p]
        rw_ref[:, sl] = _post_math(y, bonus[:, sl], g[:, sl], lnw_ref[:, sl], lnb_ref[:, sl], ones_bd)

    @pl.when(c == pl.num_programs(1) - 1)
    def _():
        sfin_ref[0] = s_ref[...]


def _chunk_call(p_main, prep_w, lnw, lnb, tri, nb, t):
    nc = t // CHUNK
    tok = lambda n, cb: pl.BlockSpec((CHUNK, n), lambda b, c: (b * nc + c, cb))
    full = lambda a: pl.BlockSpec(a.shape, lambda b, c: (0,) * a.ndim)
    return pl.pallas_call(
        _chunk_kernel,
        out_shape=(jax.ShapeDtypeStruct((nb * t, WIDTH), BF16),
                   jax.ShapeDtypeStruct((nb, PAIRS, LANES, LANES), F32)),
        grid=(nb, nc),
        in_specs=[tok(3 * WIDTH, 0), tok(4 * LANES, C_LORA // (4 * LANES))]
        + [full(a) for a in prep_w] + [full(lnw), full(lnb), full(tri)],
        out_specs=(tok(WIDTH, 0), pl.BlockSpec((1, PAIRS, LANES, LANES), lambda b, c: (b, 0, 0, 0))),
        scratch_shapes=[pltpu.VMEM((PAIRS, LANES, LANES), F32), pltpu.VMEM((1, 3 * WIDTH), F32),
                        pltpu.VMEM((1, 4 * LANES), F32)],
        compiler_params=_params(("arbitrary", "arbitrary"), 48),
        name="rwkv_chunk_scan",
    )(p_main, p_main, *prep_w, lnw, lnb, tri)


def _row_to_cols(row):
    return jnp.broadcast_to(row, (LANES, LANES)).T


def _cols_to_row(cols):
    return jnp.broadcast_to(cols, (LANES, LANES)).T[0:1, :]


def _step_kernel(s_ref, logw_ref, kap_ref, bb_ref, kt_ref, r_ref, v_ref, so_ref, y_ref):
    s = s_ref[0]
    sa = -jnp.sum(s * kap_ref[0], axis=-1, keepdims=True)
    decayed = s * jnp.exp(logw_ref[0]) + sa * bb_ref[0]
    for p in range(PAIRS):
        v_cols = _row_to_cols(v_ref[0, :, p * LANES:(p + 1) * LANES])
        ys = []
        for hh in range(2):
            h = 2 * p + hh
            s_new = decayed[h] + v_cols[hh * HEAD_DIM:(hh + 1) * HEAD_DIM, 0:HEAD_DIM] * kt_ref[0, h]
            so_ref[0, h] = s_new
            ys.append(jnp.sum(s_new * r_ref[0, h], axis=-1, keepdims=True))
        y_ref[0, :, p * LANES:(p + 1) * LANES] = _cols_to_row(jnp.concatenate(ys, axis=0))


def _step_call(state, logw, kap, bb, kt, r, v):
    nb = state.shape[0]
    row = lambda x: x.reshape(nb, HEADS, 1, HEAD_DIM)
    s_spec = pl.BlockSpec((1, HEADS, HEAD_DIM, HEAD_DIM), lambda b: (b, 0, 0, 0))
    r_spec = pl.BlockSpec((1, HEADS, 1, HEAD_DIM), lambda b: (b, 0, 0, 0))
    w_spec = pl.BlockSpec((1, 1, WIDTH), lambda b: (b, 0, 0))
    so, y = pl.pallas_call(
        _step_kernel,
        out_shape=(jax.ShapeDtypeStruct(state.shape, F32), jax.ShapeDtypeStruct((nb, 1, WIDTH), F32)),
        grid=(nb,),
        in_specs=[s_spec] + [r_spec] * 5 + [w_spec],
        out_specs=(s_spec, w_spec),
        compiler_params=_params(("arbitrary",), 32),
        name="rwkv_step",
    )(state, row(logw), row(kap), row(bb), row(kt), row(r), v.reshape(nb, 1, WIDTH))
    return so, y.reshape(nb, WIDTH)


def _sortable(x):
    x = jnp.where(x == 0.0, 0.0, x)
    bits = pltpu.bitcast(x, I32)
    return jnp.where(bits < 0, bits ^ jnp.int32(0x7FFFFFFF), bits)


def _select_topk(keys_ref, score_of, valid_of, nkc, rows, kc_size, col_of, idx_bits):
    def lane_tiles(x, op):
        out = x[:, 0:LANES]
        for q in range(1, kc_size // LANES):
            out = op(out, x[:, q * LANES:(q + 1) * LANES])
        return out

    def count(pred):
        def body(kc, acc):
            return acc + lane_tiles(jnp.where(pred(kc), 1.0, 0.0), jnp.add)
        acc = lax.fori_loop(0, nkc, body, jnp.zeros((rows, LANES), F32))
        return jnp.sum(acc, axis=-1, keepdims=True)

    kf = float(TOPK)
    thr = jnp.where(count(lambda kc: keys_ref[kc] >= 0) >= kf, jnp.int32(0), jnp.int32(INT_MIN))

    def bit_body(it, thr):
        cand = thr + lax.shift_left(jnp.int32(1), jnp.int32(30) - it)
        return jnp.where(count(lambda kc: keys_ref[kc] >= cand) >= kf, cand, thr)

    thr = lax.fori_loop(0, 31, bit_body, thr)

    def tail_body(kc, carry):
        c_gt, c_eq, lo, hi = carry
        key, sc, ok = keys_ref[kc], score_of(kc), valid_of(kc)
        gt, eq = key > thr, key == thr
        c_gt = c_gt + lane_tiles(jnp.where(gt, 1.0, 0.0), jnp.add)
        c_eq = c_eq + lane_tiles(jnp.where(eq, 1.0, 0.0), jnp.add)
        lo = jnp.minimum(lo, lane_tiles(jnp.where((gt | eq) & ok, sc, jnp.inf), jnp.minimum))
        hi = jnp.maximum(hi, lane_tiles(jnp.where(ok & ~(gt | eq), sc, -jnp.inf), jnp.maximum))
        return c_gt, c_eq, lo, hi

    zeros = jnp.zeros((rows, LANES), F32)
    c_gt, c_eq, lo, hi = lax.fori_loop(0, nkc, tail_body, (zeros, zeros, zeros + jnp.inf, zeros - jnp.inf))
    need = kf - jnp.sum(c_gt, axis=-1, keepdims=True)
    split = (jnp.sum(c_eq, axis=-1, keepdims=True) > need) & (thr > jnp.int32(INT_MIN))
    any_split = jnp.max(jnp.where(split, 1.0, 0.0)) > 0.0

    def tie_break():
        def tie_body(it, x):
            cand = x + lax.shift_left(jnp.int32(1), jnp.int32(idx_bits - 1) - it)
            c = count(lambda kc: (keys_ref[kc] == thr) & (col_of(kc) <= cand))
            return jnp.where(c < need, cand, x)

        return lax.fori_loop(0, idx_bits, tie_body, jnp.full((rows, 1), -1, I32)) + 1

    jb = lax.cond(any_split, tie_break, lambda: jnp.full((rows, 1), INT_MAX, I32))

    t_sel = jnp.where(split, jnp.inf, jnp.min(lo, axis=-1, keepdims=True))
    t_uns = jnp.max(hi, axis=-1, keepdims=True)
    thr_bits = jnp.where(split, thr, jnp.int32(INT_MAX))

    def chosen(kc):
        sc, key = score_of(kc), keys_ref[kc]
        nearer = (sc - t_uns) > (t_sel - sc)
        by_bits = (key > thr_bits) | ((key == thr_bits) & (col_of(kc) <= jb))
        return (nearer | by_bits) & valid_of(kc)

    return chosen


TQ = 256
KC = 512


M_INIT = -1e30


def _idx_key_pack(kidx):
    hi, lo = _split(kidx)
    return jnp.concatenate([hi, lo, hi, jnp.zeros_like(hi)], axis=1)


def _attn_kernel(qidx_ref, widx_ref, kcat_ref, aq_ref, kt_ref, vt_ref,
                 att_ref, keys_ref, bias_ref, qcat_ref, d_ref, qm_ref, m_ref, l_ref, acc_ref):
    i = pl.program_id(1)
    nkc = lax.shift_right_logical((i + 1) * TQ + (KC - 1), int(np.log2(KC)))
    row = i * TQ + lax.broadcasted_iota(I32, (TQ, KC), 0)
    lane_kc = lax.broadcasted_iota(I32, (TQ, KC), 1)
    col_of = lambda kc: kc * KC + lane_kc
    head_rows = lambda h: slice(h * TQ, (h + 1) * TQ)
    first = lax.broadcasted_iota(I32, (1, LANES), 1) < HEAD_DIM

    for p in range(PAIRS):
        x = qidx_ref[:, p * LANES:(p + 1) * LANES]
        hi = x.astype(BF16).astype(F32)
        lo = x - hi
        hi_sw = pltpu.roll(hi, HEAD_DIM, 1)
        lo_sw = pltpu.roll(lo, HEAD_DIM, 1)
        for hh in range(2):
            rows = head_rows(2 * p + hh)
            own_hi, other_hi = (hi, hi_sw) if hh == 0 else (hi_sw, hi)
            qcat_ref[rows, 0:LANES] = jnp.where(first, own_hi, other_hi).astype(BF16)
            qcat_ref[rows, LANES:2 * LANES] = jnp.where(first, lo if hh == 0 else lo_sw, 0.0).astype(BF16)
    wv = widx_ref[:, IDX_DIM:IDX_DIM + HEADS] * (HEADS ** -0.5 * IDX_DIM ** -0.5)

    def score_body(kc, carry):
        d_ref[...] = _dot(qcat_ref[...], kcat_ref[kc])
        acc = jnp.zeros((TQ, KC), F32)
        for h in range(HEADS):
            acc = acc + jnp.maximum(d_ref[head_rows(h), :], 0.0) * wv[:, h:h + 1]
        keys_ref[kc] = jnp.where(col_of(kc) <= row, _sortable(acc), jnp.int32(INT_MIN))
        bias_ref[kc] = acc
        return carry

    lax.fori_loop(0, nkc, score_body, 0)

    chosen = _select_topk(keys_ref, lambda kc: bias_ref[kc], lambda kc: col_of(kc) <= row, nkc, TQ, KC, col_of, 12)

    def bias_body(kc, carry):
        bias_ref[kc] = jnp.where(chosen(kc), 0.0, 2.0 * M_INIT)
        return carry

    lax.fori_loop(0, nkc, bias_body, 0)

    for p in range(PAIRS):
        aq = aq_ref[:, p * LANES:(p + 1) * LANES] * (HEAD_DIM ** -0.5)
        qm_ref[p, 0:TQ, :] = jnp.where(first, aq, 0.0).astype(BF16)
        qm_ref[p, TQ:2 * TQ, :] = jnp.where(first, 0.0, aq).astype(BF16)
    def lane_tiles(x, op):
        out = x[:, 0:LANES]
        for q in range(1, KC // LANES):
            out = op(out, x[:, q * LANES:(q + 1) * LANES])
        return out

    halves = (slice(0, TQ), slice(TQ, 2 * TQ))

    m_ref[...] = jnp.full(m_ref.shape, M_INIT, F32)
    l_ref[...] = jnp.zeros(l_ref.shape, F32)
    acc_ref[...] = jnp.zeros(acc_ref.shape, F32)

    def pv_body(kc, carry):
        bias = bias_ref[kc]
        sls = [slice(p * LANES, (p + 1) * LANES) for p in range(PAIRS)]
        s = [_dot(qm_ref[p], kt_ref[kc, sls[p], :]) for p in range(PAIRS)]
        es = []
        for p in range(PAIRS):
            e = []
            for rs in halves:
                sh = s[p][rs] + bias
                m_old = m_ref[p, rs, :]
                m_new = jnp.maximum(m_old, jnp.max(lane_tiles(sh, jnp.maximum), axis=-1, keepdims=True))
                alpha = jnp.exp(m_old - m_new)
                x = jnp.exp(sh - m_new)
                m_ref[p, rs, :] = m_new
                l_ref[p, rs, :] = alpha * l_ref[p, rs, :] + lane_tiles(x, jnp.add)
                acc_ref[p, rs, :] = alpha * acc_ref[p, rs, :]
                e.append(x.astype(BF16))
            es.append(jnp.concatenate(e, axis=0))
        pv = [_dot(es[p], vt_ref[kc, sls[p], :], _NT) for p in range(PAIRS)]
        for p in range(PAIRS):
            acc_ref[p] += pv[p]
        return carry

    lax.fori_loop(0, nkc, pv_body, 0)

    for p in range(PAIRS):
        o = acc_ref[p] / jnp.sum(l_ref[p], axis=-1, keepdims=True)
        att_ref[:, p * LANES:(p + 1) * LANES] = jnp.where(first, o[0:TQ], o[TQ:2 * TQ]).astype(BF16)


def _attn_call(p_idx, kcat_t, p_main, kt, vt, nb, t):
    nq = t // TQ
    nkc = t // KC
    qtok = lambda n, cb: pl.BlockSpec((TQ, n), lambda b, i: (b * nq + i, cb))
    seq3 = lambda n: pl.BlockSpec((nkc, n, KC), lambda b, i: (b, 0, 0), pipeline_mode=pl.Buffered(1))
    return pl.pallas_call(
        _attn_kernel,
        out_shape=jax.ShapeDtypeStruct((nb * t, WIDTH), BF16),
        grid=(nb, nq),
        in_specs=[qtok(WIDTH, 0), qtok(LANES, WIDTH // LANES), seq3(4 * IDX_DIM),
                  qtok(WIDTH, C_AQ // WIDTH), seq3(WIDTH), seq3(WIDTH)],
        out_specs=qtok(WIDTH, 0),
        scratch_shapes=[pltpu.VMEM((nkc, TQ, KC), I32), pltpu.VMEM((nkc, TQ, KC), F32),
                        pltpu.VMEM((HEADS * TQ, 2 * LANES), BF16), pltpu.VMEM((HEADS * TQ, KC), F32),
                        pltpu.VMEM((PAIRS, 2 * TQ, LANES), BF16), pltpu.VMEM((PAIRS, 2 * TQ, 1), F32),
                        pltpu.VMEM((PAIRS, 2 * TQ, LANES), F32), pltpu.VMEM((PAIRS, 2 * TQ, LANES), F32)],
        compiler_params=_params(("arbitrary", "arbitrary"), 56),
        name="dsa_prompt_attention",
    )(p_idx, p_idx, kcat_t, p_main, kt, vt)


PG = 16
SROWS = 72
SCOLS = SROWS * PAGE


PG_IDX = 16


def _sidx_kernel(pt_ref, q_ref, w_ref, knew_ref, *rest, npages):
    pages, o_ref = rest[:PG_IDX], rest[PG_IDX]
    j = pl.program_id(1)
    qh, ql = _split(q_ref[0])
    w = w_ref[0] * (HEADS ** -0.5 * IDX_DIM ** -0.5)

    def score(keys):
        kh, kl = _split(keys)
        d = _dot(qh, kh) + (_dot(qh, kl) + _dot(ql, kh))
        return jnp.sum(jnp.maximum(d, 0.0) * w, axis=0, keepdims=True)

    @pl.when(j == 0)
    def _():
        o_ref[0] = jnp.zeros((SROWS, PAGE), F32)
        o_ref[0, npages:npages + 1, :] = score(knew_ref[0])

    sc = score(jnp.concatenate([pages[g][0] for g in range(PG_IDX)], axis=1))
    for g in range(PG_IDX):
        o_ref[0, pl.ds(j * PG_IDX + g, 1), :] = sc[:, g * PAGE:(g + 1) * PAGE]


def _sidx_call(page_table, q, w, knew, cache_kidx):
    nb, npages = page_table.shape
    one = lambda s: pl.BlockSpec((1,) + s, lambda b, j, pt: (b, 0, 0))
    page = lambda g: pl.BlockSpec((1, IDX_DIM, PAGE), lambda b, j, pt: (pt[b, j * PG_IDX + g], 0, 0))
    return pl.pallas_call(
        functools.partial(_sidx_kernel, npages=npages),
        out_shape=jax.ShapeDtypeStruct((nb, SROWS, PAGE), F32),
        grid_spec=pltpu.PrefetchScalarGridSpec(
            num_scalar_prefetch=1,
            grid=(nb, npages // PG_IDX),
            in_specs=[one((HEADS, IDX_DIM)), one((HEADS, 1)), one((IDX_DIM, PAGE))]
            + [page(g) for g in range(PG_IDX)],
            out_specs=one((SROWS, PAGE)),
        ),
        compiler_params=_params(("arbitrary", "arbitrary"), 32),
        name="dsa_sample_index_scores",
    )(page_table, q, w, knew, *([cache_kidx] * PG_IDX))


SKC = 512
SNKC = SCOLS // SKC


def _ssel_kernel(sc_ref, bias_ref, keys_ref, chunks_ref, *, n_valid):
    rows = sc_ref.shape[0]
    lane_kc = lax.broadcasted_iota(I32, (rows, SKC), 1)
    col_of = lambda kc: kc * SKC + lane_kc
    valid_of = lambda kc: col_of(kc) < n_valid
    for c in range(SNKC):
        x = sc_ref[:, c * SKC:(c + 1) * SKC]
        chunks_ref[c] = x
        keys_ref[c] = jnp.where(valid_of(c), _sortable(x), jnp.int32(INT_MIN))
    chosen = _select_topk(keys_ref, lambda kc: chunks_ref[kc], valid_of, SNKC, rows, SKC, col_of, 14)
    for c in range(SNKC):
        bias_ref[:, c * SKC:(c + 1) * SKC] = jnp.where(chosen(c), 0.0, NEG)


def _ssel_call(scores, n_valid):
    rows = scores.shape[0]
    return pl.pallas_call(
        functools.partial(_ssel_kernel, n_valid=n_valid),
        out_shape=jax.ShapeDtypeStruct(scores.shape, F32),
        scratch_shapes=[pltpu.VMEM((SNKC, rows, SKC), I32), pltpu.VMEM((SNKC, rows, SKC), F32)],
        name="dsa_sample_select",
    )(scores)


def _sattn_kernel(pt_ref, q_ref, kn_ref, vn_ref, bias_ref, *rest, npages):
    kpages, vpages = rest[:PG], rest[PG:2 * PG]
    o_ref, qb_ref, m_ref, l_ref, acc_ref = rest[2 * PG:]
    j = pl.program_id(1)
    scale = HEAD_DIM ** -0.5

    @pl.when(j == 0)
    def _():
        on = bias_ref[0, npages:npages + 1, 0:1] == 0.0
        first = on & (lax.broadcasted_iota(I32, (1, PAGE), 1) == 0)
        for p in range(PAIRS):
            sl = slice(p * LANES, (p + 1) * LANES)
            q_cols = _row_to_cols(q_ref[0, :, sl] * scale)
            k_cols = _row_to_cols(kn_ref[0, :, sl])
            v_cols = _row_to_cols(vn_ref[0, :, sl])
            for hh in range(2):
                h = 2 * p + hh
                rows = slice(hh * HEAD_DIM, (hh + 1) * HEAD_DIM)
                qb_ref[h] = q_cols[rows]
                s_new = jnp.sum(q_cols[rows] * k_cols[rows], axis=0, keepdims=True)
                m_ref[h] = jnp.where(on, s_new, NEG)
                l_ref[h] = jnp.broadcast_to(jnp.where(on, 1.0, 0.0), (1, PAGE))
                acc_ref[h] = jnp.where(first, v_cols[rows], 0.0)

    for g in range(PG):
        on = bias_ref[0, pl.ds(j * PG + g, 1), :] == 0.0
        for h in range(HEADS):
            s = jnp.sum(qb_ref[h] * kpages[g][0, h], axis=0, keepdims=True)
            m = m_ref[h]
            m_new = jnp.maximum(m, jnp.max(jnp.where(on, s, NEG), axis=-1, keepdims=True))
            e = jnp.where(on, jnp.exp(s - m_new), 0.0)
            alpha = jnp.exp(m - m_new)
            m_ref[h] = m_new
            l_ref[h] = alpha * l_ref[h] + jnp.sum(e, axis=-1, keepdims=True)
            acc_ref[h] = alpha * acc_ref[h] + e * vpages[g][0, h]

    @pl.when(j == pl.num_programs(1) - 1)
    def _():
        for p in range(PAIRS):
            outs = [jnp.sum(acc_ref[h], axis=-1, keepdims=True) / l_ref[h][:, 0:1] for h in (2 * p, 2 * p + 1)]
            o_ref[0, :, p * LANES:(p + 1) * LANES] = _cols_to_row(jnp.concatenate(outs, axis=0))


def _sattn_call(page_table, q, kn, vn, bias, ck, cv):
    nb, npages = page_table.shape
    one = lambda s: pl.BlockSpec((1,) + s, lambda b, j, pt: (b,) + (0,) * len(s))
    col = one((1, WIDTH))
    page = lambda g: pl.BlockSpec((1, HEADS, HEAD_DIM, PAGE), lambda b, j, pt: (pt[b, j * PG + g], 0, 0, 0))
    return pl.pallas_call(
        functools.partial(_sattn_kernel, npages=npages),
        out_shape=jax.ShapeDtypeStruct((nb, 1, WIDTH), F32),
        grid_spec=pltpu.PrefetchScalarGridSpec(
            num_scalar_prefetch=1,
            grid=(nb, npages // PG),
            in_specs=[col, col, col, one((SROWS, PAGE))] + [page(g) for g in range(PG)] * 2,
            out_specs=col,
            scratch_shapes=[pltpu.VMEM((HEADS, HEAD_DIM, PAGE), F32), pltpu.VMEM((HEADS, 1, PAGE), F32),
                            pltpu.VMEM((HEADS, 1, PAGE), F32), pltpu.VMEM((HEADS, HEAD_DIM, PAGE), F32)],
        ),
        compiler_params=_params(("arbitrary", "arbitrary"), 48),
        name="dsa_sample_attention",
    )(page_table, q, kn, vn, bias, *([ck] * PG), *([cv] * PG))


def _merge_kernel(rw_ref, att_ref, wr_ref, wa_ref, ga_ref, gb_ref, o_ref):
    yr = _dot(rw_ref[...], wr_ref[...])
    ya = _dot(att_ref[...], wa_ref[...])
    o_ref[...] = (_sigmoid(ga_ref[...]) * yr + _sigmoid(gb_ref[...]) * ya).astype(BF16)


def _merge_call(rw, att, w_r, w_a, p_main, tm, tn):
    m = rw.shape[0]
    lhs = pl.BlockSpec((tm, WIDTH), lambda j, i: (i, 0))
    rhs = pl.BlockSpec((WIDTH, tn), lambda j, i: (0, j))
    gate = lambda c0: pl.BlockSpec((tm, tn), lambda j, i: (i, c0 // tn + j))
    return pl.pallas_call(
        _merge_kernel,
        out_shape=jax.ShapeDtypeStruct((m, D_MODEL), BF16),
        grid=(D_MODEL // tn, m // tm),
        in_specs=[lhs, lhs, rhs, rhs, gate(C_GA), gate(C_GB)],
        out_specs=pl.BlockSpec((tm, tn), lambda j, i: (i, j)),
        compiler_params=_params(("arbitrary", "arbitrary"), 32),
        name="mixer_merge",
    )(rw, att, w_r, w_a, p_main, p_main)


def _rms(x, g):
    return x * lax.rsqrt(jnp.mean(x * x, axis=-1, keepdims=True) + RMS_EPS) * g


def _outproj_kernel(m_ref, w_ref, x_ref, gpost_ref, gpre_ref, h_ref, f_ref):
    h = x_ref[...] + _rms(_dot(m_ref[...], w_ref[...]), gpost_ref[...])
    h_ref[...] = h
    f_ref[...] = _rms(h, gpre_ref[...]).astype(BF16)


def _outproj_call(mix, w_out, x, g_post, g_pre, tm):
    m = x.shape[0]
    tok = pl.BlockSpec((tm, D_MODEL), lambda i: (i, 0))
    full = lambda a: pl.BlockSpec(a.shape, lambda i: (0, 0))
    return pl.pallas_call(
        _outproj_kernel,
        out_shape=(jax.ShapeDtypeStruct((m, D_MODEL), F32), jax.ShapeDtypeStruct((m, D_MODEL), BF16)),
        grid=(m // tm,),
        in_specs=[tok, full(w_out), tok, full(g_post), full(g_pre)],
        out_specs=(tok, tok),
        compiler_params=_params(("arbitrary",), 40),
        name="mixer_out_proj",
    )(mix, w_out, x, g_post, g_pre)


def _ffn_kernel(f_ref, wg_ref, wu_ref, wd_ref, h_ref, gp_ref, y_ref, acc_ref):
    j = pl.program_id(1)

    @pl.when(j == 0)
    def _():
        acc_ref[...] = jnp.zeros(acc_ref.shape, F32)

    f = f_ref[...]
    gate = _dot(f, wg_ref[...])
    t = (gate * _sigmoid(gate) * _dot(f, wu_ref[...])).astype(BF16)
    acc_ref[...] += _dot(t, wd_ref[...])

    @pl.when(j == pl.num_programs(1) - 1)
    def _():
        y_ref[...] = h_ref[...] + _rms(acc_ref[...], gp_ref[...])


def _ffn_call(f, w_gate, w_up, w_down, h, g_post, tm, tf):
    m = f.shape[0]
    d_ff = w_gate.shape[1]
    tok = pl.BlockSpec((tm, D_MODEL), lambda i, j: (i, 0))
    return pl.pallas_call(
        _ffn_kernel,
        out_shape=jax.ShapeDtypeStruct((m, D_MODEL), F32),
        grid=(m // tm, d_ff // tf),
        in_specs=[tok, pl.BlockSpec((D_MODEL, tf), lambda i, j: (0, j)), pl.BlockSpec((D_MODEL, tf), lambda i, j: (0, j)),
                  pl.BlockSpec((tf, D_MODEL), lambda i, j: (j, 0)), tok, pl.BlockSpec((1, D_MODEL), lambda i, j: (0, 0))],
        out_specs=tok,
        scratch_shapes=[pltpu.VMEM((tm, D_MODEL), F32)],
        compiler_params=_params(("arbitrary", "arbitrary"), 48),
        name="swiglu_ffn",
    )(f, w_gate, w_up, w_down, h, g_post)


def _pad_cols(a, n):
    return jnp.pad(a, ((0, 0), (0, n - a.shape[1])))


def _regroup_cols(a):
    w3, ls = 3 * WIDTH, LORA_SMALL
    lora = jnp.concatenate([_pad_cols(a[:, w3:w3 + ls], LANES), _pad_cols(a[:, w3 + ls:w3 + 2 * ls], LANES),
                            a[:, w3 + 2 * ls:]], axis=1)
    return a[:, :w3], lora


def _ungroup_cols(rkv, lora):
    ls = LORA_SMALL
    return jnp.concatenate([rkv, lora[:, :ls], lora[:, LANES:LANES + ls], lora[:, 2 * LANES:]], axis=1)


def kernel(x_prompt, x_sample, cache_k, cache_v, cache_kidx, state_wkv, state_shift, page_table, g_pre_mix, w_in,
           rwkv_mu, w0, w_w2, a0, w_a2, w_g2, k_k, k_a, r_k, ln_x_w, ln_x_b, w_o_rwkv, w_o_attn, w_out,
           g_post_mix, g_pre_ffn, w_gate, w_up, w_down, g_post_ffn):
    nb, t, _ = x_prompt.shape
    ns = x_sample.shape[0]
    row = lambda a: a.reshape(1, -1).astype(F32)

    o_q = RWKV_COLS
    o_qi = o_q + 3 * WIDTH
    o_ki = o_qi + WIDTH
    o_ga = o_ki + IDX_DIM + HEADS
    w_rkv, w_lora = _regroup_cols(w_in[:, :RWKV_COLS])
    o_k = o_q + WIDTH
    w_main = jnp.concatenate([w_rkv, w_in[:, o_q:o_k], w_in[:, o_ga:], w_lora], axis=1).astype(BF16)
    w_kv = w_in[:, o_k:o_qi].astype(BF16)
    w_kt, w_vt = w_kv[:, :WIDTH].T, w_kv[:, WIDTH:].T
    w_idx_hi, w_idx_lo = _split(jnp.concatenate([w_in[:, o_qi:o_ki], _pad_cols(w_in[:, o_ki:o_ga], LANES)], axis=1))
    mu_rkv, mu_lora = _regroup_cols(row(rwkv_mu))
    pad_rows = lambda a: jnp.pad(a, ((0, LANES - a.shape[0]), (0, 0))).astype(BF16)
    lane = np.arange(LANES)
    ones_bd = jnp.asarray((lane[:, None] // HEAD_DIM) == (lane[None, :] // HEAD_DIM), BF16)
    tri = jnp.asarray(np.arange(CHUNK)[:, None] >= np.arange(CHUNK)[None, :], BF16)
    prep_w = [mu_rkv, mu_lora, row(w0), row(a0), row(k_k), row(k_a), row(r_k), pad_rows(w_w2), pad_rows(w_a2),
              w_g2.astype(BF16), ones_bd]
    lnw, lnb = row(ln_x_w), row(ln_x_b)
    w_or, w_oa, w_o = w_o_rwkv.astype(BF16), w_o_attn.astype(BF16), w_out.astype(BF16)
    w_g, w_u, w_d = w_gate.astype(BF16), w_up.astype(BF16), w_down.astype(BF16)
    gains = [row(g_pre_mix), row(g_post_mix), row(g_pre_ffn), row(g_post_ffn)]

    def project(x2d, tm, transposed):
        u = _rms_call(x2d, gains[0], min(tm, 512), transposed)
        p_main = _mm_call([u[0]], [w_main], [(0, 0)], F32, min(tm, 512), N_MAIN // 4, "in_proj")
        p_idx = _mm_call([u[0], u[1]], [w_idx_hi, w_idx_lo], [(0, 0), (0, 1), (1, 0)], F32, tm, 384, "in_proj_idx")
        return p_main, p_idx, u

    def finish(x2d, p_main, rw, att, tm, tn, tm_ffn):
        mix = _merge_call(rw, att, w_or, w_oa, p_main, tm, tn)
        h, f = _outproj_call(mix, w_o, x2d, gains[1], gains[2], min(tm, 256))
        return _ffn_call(f, w_g, w_u, w_d, h, gains[3], tm_ffn, 512)

    m = nb * t
    xp = x_prompt.reshape(m, D_MODEL)
    p_main, p_idx, u_p = project(xp, 1024, True)
    rw, s_fin = _chunk_call(p_main, prep_w, lnw, lnb, tri, nb, t)
    kidx = p_idx[:, WIDTH:WIDTH + IDX_DIM]
    k_t, k_tc = _proj_t_call(w_kt, u_p[2], nb, t, KC)
    v_t, v_tc = _proj_t_call(w_vt, u_p[2], nb, t, KC)
    chunk_t = lambda a: jnp.swapaxes(a.reshape(m // KC, KC, a.shape[1]), 1, 2)
    att = _attn_call(p_idx, chunk_t(_idx_key_pack(kidx)), p_main, k_tc, v_tc, nb, t)
    tok_major = lambda a: jnp.transpose(a.reshape(nb, HEADS, HEAD_DIM, t), (0, 3, 1, 2))
    k_p, v_p = tok_major(k_t), tok_major(v_t)
    y_p = finish(xp, p_main, rw, att, 1024, 512, 512)
    blocks = jnp.stack([s_fin[:, :, :HEAD_DIM, :HEAD_DIM], s_fin[:, :, HEAD_DIM:, HEAD_DIM:]], axis=2)
    wkv_p = jnp.swapaxes(blocks.reshape(nb, HEADS, HEAD_DIM, HEAD_DIM), -1, -2)
    last = p_main.reshape(nb, t, N_MAIN)[:, -1]
    shift_p = _ungroup_cols(last[:, :3 * WIDTH], last[:, C_LORA:])

    xs = x_sample.reshape(ns, D_MODEL)
    ps_main, ps_idx, u_s = project(xs, ns, False)
    ps_kv = _mm_call([u_s[0]], [w_kv], [(0, 0)], F32, ns, 512, "in_proj_kv")
    prev_rkv, prev_lora = _regroup_cols(state_shift)
    sf = _prep_call(ps_main, prev_rkv, prev_lora, lax.optimization_barrier(prep_w))
    logw_s, kap_s, bb_s, kt_s, r_s, v_s, bonus_s, g_s = sf
    wkv_s, y_s = _step_call(state_wkv, logw_s, kap_s, bb_s, kt_s, r_s, v_s)
    rw_s = _post_call(y_s, bonus_s, g_s, lnw, lnb, ones_bd)

    npages = page_table.shape[1]
    past = npages * PAGE
    q_i = ps_idx[:, :WIDTH].reshape(ns, HEADS, IDX_DIM)
    w_i = ps_idx[:, WIDTH + IDX_DIM:WIDTH + IDX_DIM + HEADS].reshape(ns, HEADS, 1)
    kidx_s = ps_idx[:, WIDTH:WIDTH + IDX_DIM]
    knew = jnp.pad(kidx_s[:, :, None], ((0, 0), (0, 0), (0, PAGE - 1)))
    scores = _sidx_call(page_table, q_i, w_i, knew, jnp.transpose(cache_kidx, (0, 2, 1)))
    bias = _ssel_call(scores.reshape(ns, SCOLS), past + 1).reshape(ns, SROWS, PAGE)
    k_s, v_s_att = ps_kv[:, :WIDTH], ps_kv[:, WIDTH:]
    seq = lambda a: a.reshape(ns, 1, WIDTH)
    att_s = _sattn_call(page_table, seq(ps_main[:, C_AQ:C_AQ + WIDTH]), seq(k_s), seq(v_s_att), bias,
                        jnp.transpose(cache_k, (0, 2, 3, 1)), jnp.transpose(cache_v, (0, 2, 3, 1)))
    y_smp = finish(xs, ps_main, rw_s, att_s.reshape(ns, WIDTH).astype(BF16), ns, 512, ns)
    shift_s = _ungroup_cols(ps_main[:, :3 * WIDTH], ps_main[:, C_LORA:])

    hd = (HEADS, HEAD_DIM)
    return (y_p.reshape(nb, t, D_MODEL), y_smp.reshape(ns, 1, D_MODEL),
            k_p, v_p, kidx.reshape(nb, t, IDX_DIM),
            wkv_p, shift_p,
            k_s.reshape(ns, 1, *hd), v_s_att.reshape(ns, 1, *hd), kidx_s.reshape(ns, 1, IDX_DIM),
            wkv_s, shift_s)
```
